```python
import math
import jax, jax.numpy as jnp
from jax import lax
import numpy as np

D_MODEL = 1024
BATCH = 4
SEQ = 8192
DEPTH = 1

CHUNK = 64
N_META = 16
PAD = CHUNK - N_META
EPS = 1e-6
ROPE_BASE = 10000.0

RET_HEADS = 8
RET_DK = 128
RET_DV = 128
RET_QK = RET_HEADS * RET_DK
RET_V = RET_HEADS * RET_DV

SSD_D_INNER = D_MODEL
SSD_HEAD_DIM = 64
SSD_HEADS = SSD_D_INNER // SSD_HEAD_DIM
SSD_GROUPS = 4
SSD_HPG = SSD_HEADS // SSD_GROUPS
SSD_STATE = 128
SSD_CONV = 4
SSD_CONV_DIM = SSD_D_INNER + 2 * SSD_GROUPS * SSD_STATE

MIX_WIDTH = RET_V + SSD_D_INNER
IN_COLS = 2 * RET_QK + 2 * RET_V + SSD_D_INNER + SSD_CONV_DIM + SSD_HEADS
D_FF = 4 * D_MODEL

kernel_name = "hybrid_retention_ssd_meta_block"


def _rmsnorm(x, w):
    xf = x.astype(jnp.float32)
    y = xf * lax.rsqrt(jnp.mean(xf * xf, axis=-1, keepdims=True) + EPS)
    return y.astype(x.dtype) * w


def _split_cols(p):
    sizes = (RET_QK, RET_QK, RET_V, RET_V, SSD_D_INNER, SSD_CONV_DIM, SSD_HEADS)
    outs, start = [], 0
    for s in sizes:
        outs.append(p[..., start:start + s])
        start += s
    return outs


def _rope(t, pos):
    half = t.shape[-1] // 2
    freqs = ROPE_BASE ** (-jnp.arange(0, half, dtype=jnp.float32) / half)
    ang = pos[:, None] * freqs[None, :]
    cos = jnp.cos(ang)[None, :, None, :]
    sin = jnp.sin(ang)[None, :, None, :]
    t1, t2 = t[..., :half], t[..., half:]
    return jnp.concatenate([t1 * cos - t2 * sin, t1 * sin + t2 * cos], axis=-1)


def _retention(q, k, v, g, pos, ret_norm_w):
    b, l, _ = q.shape
    nc = l // CHUNK
    q = q.reshape(b, l, RET_HEADS, RET_DK)
    k = k.reshape(b, l, RET_HEADS, RET_DK)
    q = _rope(q, pos) * (RET_DK ** -0.5)
    k = _rope(k, pos)
    q = q.reshape(b, nc, CHUNK, RET_HEADS, RET_DK)
    k = k.reshape(b, nc, CHUNK, RET_HEADS, RET_DK)
    vv = v.reshape(b, nc, CHUNK, RET_HEADS, RET_DV)

    log_gamma = jnp.log1p(-jnp.exp2(-5.0 - jnp.arange(RET_HEADS, dtype=jnp.float32)))
    idx = jnp.arange(CHUNK, dtype=jnp.float32)
    dist = jnp.abs(idx[:, None] - idx[None, :])
    decay_intra = jnp.exp(log_gamma[:, None, None] * dist[None])

    scores = jnp.einsum('bcnhd,bcmhd->bhcnm', q, k) * decay_intra[None, :, None]
    o_intra = jnp.einsum('bhcnm,bcmhe->bcnhe', scores, vv)

    k_dec = k * jnp.exp((CHUNK - 1.0 - idx)[:, None] * log_gamma[None, :])[None, None, :, :, None]
    kv = jnp.einsum('bcmhd,bcmhe->cbhde', k_dec, vv).astype(jnp.float32)
    chunk_decay = jnp.exp(CHUNK * log_gamma)[None, :, None, None]

    def step(s, kv_c):
        return chunk_decay * s + kv_c, s

    _, s_prev = lax.scan(step, jnp.zeros((b, RET_HEADS, RET_DK, RET_DV), jnp.float32), kv)
    q_dec = q * jnp.exp((idx + 1.0)[:, None] * log_gamma[None, :])[None, None, :, :, None]
    o_inter = jnp.einsum('bcnhd,cbhde->bcnhe', q_dec, s_prev)

    o = (o_intra + o_inter).reshape(b, l, RET_HEADS, RET_DV).astype(jnp.float32)
    o = o * lax.rsqrt(jnp.mean(o * o, axis=-1, keepdims=True) + EPS)
    o = o.reshape(b, l, RET_V).astype(v.dtype) * ret_norm_w
    return o * jax.nn.silu(g)


def _ssd(z, xbc, dt_raw, valid, conv_w, conv_b, dt_bias, a_log, d_skip, norm_w):
    b, l, _ = z.shape
    nc = l // CHUNK
    conv = lax.conv_general_dilated(
        xbc, conv_w[:, None, :].astype(xbc.dtype), window_strides=(1,),
        padding=[(SSD_CONV - 1, 0)], dimension_numbers=('NWC', 'WIO', 'NWC'),
        feature_group_count=SSD_CONV_DIM)
    xbc = jax.nn.silu(conv + conv_b) * valid[None, :, None]
    xs = xbc[..., :SSD_D_INNER]
    bm = xbc[..., SSD_D_INNER:SSD_D_INNER + SSD_GROUPS * SSD_STATE]
    cm = xbc[..., SSD_D_INNER + SSD_GROUPS * SSD_STATE:]
    dt = jax.nn.softplus((dt_raw + dt_bias).astype(jnp.float32)) * valid[None, :, None]
    a = -jnp.exp(a_log.astype(jnp.float32))

    xs = xs.reshape(b, nc, CHUNK, SSD_GROUPS, SSD_HPG, SSD_HEAD_DIM)
    bm = bm.reshape(b, nc, CHUNK, SSD_GROUPS, SSD_STATE)
    cm = cm.reshape(b, nc, CHUNK, SSD_GROUPS, SSD_STATE)
    dt = dt.reshape(b, nc, CHUNK, SSD_GROUPS, SSD_HPG)
    a_cs = jnp.cumsum(dt * a.reshape(SSD_GROUPS, SSD_HPG), axis=2)
    xdt = xs * dt[..., None]

    seg = a_cs[:, :, :, None] - a_cs[:, :, None, :]
    causal = jnp.tril(jnp.ones((CHUNK, CHUNK), bool))[None, None, :, :, None, None]
    lmat = jnp.exp(jnp.where(causal, seg, -jnp.inf))
    cb = jnp.einsum('bclgn,bcsgn->bclsg', cm, bm)
    y_diag = jnp.einsum('bclsgr,bcsgrp->bclgrp', cb[..., None] * lmat, xdt)

    decay_states = jnp.exp(a_cs[:, :, -1:] - a_cs)
    states = jnp.einsum('bcsgn,bcsgrp->cbgrpn', bm, xdt * decay_states[..., None]).astype(jnp.float32)
    chunk_decay = jnp.transpose(jnp.exp(a_cs[:, :, -1]), (1, 0, 2, 3))[..., None, None]

    def step(h, inp):
        s_c, d_c = inp
        return d_c * h + s_c, h

    h0 = jnp.zeros((b, SSD_GROUPS, SSD_HPG, SSD_HEAD_DIM, SSD_STATE), jnp.float32)
    _, h_prev = lax.scan(step, h0, (states, chunk_decay))
    y_off = jnp.einsum('bclgn,cbgrpn->bclgrp', cm, h_prev) * jnp.exp(a_cs)[..., None]

    y = y_diag + y_off + xs * d_skip.reshape(SSD_GROUPS, SSD_HPG)[..., None]
    y = y.reshape(b, l, SSD_D_INNER) * jax.nn.silu(z)
    yf = y.astype(jnp.float32).reshape(b, l, SSD_GROUPS, SSD_D_INNER // SSD_GROUPS)
    yf = yf * lax.rsqrt(jnp.mean(yf * yf, axis=-1, keepdims=True) + EPS)
    return yf.reshape(b, l, SSD_D_INNER).astype(z.dtype) * norm_w


def setup_inputs(seed: int = 0) -> dict:
    key = jax.random.key(seed)
    ks = jax.random.split(key, 16)
    f32 = jnp.float32
    dt0 = jnp.exp(jax.random.uniform(ks[7], (DEPTH, SSD_HEADS), f32, math.log(1e-3), math.log(1e-1)))
    return {
        "x": jax.random.normal(ks[0], (BATCH, SEQ, D_MODEL), f32),
        "meta_tokens": jax.random.normal(ks[1], (N_META, D_MODEL), f32),
        "norm1_w": 1.0 + 0.02 * jax.random.normal(ks[2], (DEPTH, D_MODEL), f32),
        "w_in": jax.random.normal(ks[3], (DEPTH, D_MODEL, IN_COLS), f32) * D_MODEL ** -0.5,
        "ret_norm_w": 1.0 + 0.02 * jax.random.normal(ks[4], (DEPTH, RET_V), f32),
        "conv_w": jax.random.normal(ks[5], (DEPTH, SSD_CONV, SSD_CONV_DIM), f32) * SSD_CONV ** -0.5,
        "conv_b": 0.01 * jax.random.normal(ks[6], (DEPTH, SSD_CONV_DIM), f32),
        "dt_bias": dt0 + jnp.log(-jnp.expm1(-dt0)),
        "a_log": jnp.log(jax.random.uniform(ks[8], (DEPTH, SSD_HEADS), f32, 1.0, 16.0)),
        "d_skip": 1.0 + 0.02 * jax.random.normal(ks[9], (DEPTH, SSD_HEADS), f32),
        "ssd_norm_w": 1.0 + 0.02 * jax.random.normal(ks[10], (DEPTH, SSD_D_INNER), f32),
        "w_out": jax.random.normal(ks[11], (DEPTH, MIX_WIDTH, D_MODEL), f32) * MIX_WIDTH ** -0.5,
        "norm2_w": 1.0 + 0.02 * jax.random.normal(ks[12], (DEPTH, D_MODEL), f32),
        "w_ff1": jax.random.normal(ks[13], (DEPTH, D_MODEL, D_FF), f32) * D_MODEL ** -0.5,
        "w_ff2": jax.random.normal(ks[14], (DEPTH, D_FF, D_MODEL), f32) * D_FF ** -0.5,
        "final_norm_w": 1.0 + 0.02 * jax.random.normal(ks[15], (D_MODEL,), f32),
    }


def reference(x, meta_tokens, norm1_w, w_in, ret_norm_w, conv_w, conv_b, dt_bias, a_log,
              d_skip, ssd_norm_w, w_out, norm2_w, w_ff1, w_ff2, final_norm_w):
    b = x.shape[0]
    h = jnp.concatenate([
        jnp.zeros((b, PAD, D_MODEL), x.dtype),
        jnp.broadcast_to(meta_tokens.astype(x.dtype)[None], (b, N_META, D_MODEL)),
        x], axis=1)
    l = h.shape[1]
    idx = jnp.arange(l)
    valid = (idx >= PAD).astype(x.dtype)
    pos = (idx - PAD).astype(jnp.float32)

    for i in range(DEPTH):
        hn = _rmsnorm(h, norm1_w[i])
        proj = (hn @ w_in[i]) * valid[None, :, None]
        q, k, v, g, z, xbc, dt_raw = _split_cols(proj)
        y_ret = _retention(q, k, v, g, pos, ret_norm_w[i])
        y_ssd = _ssd(z, xbc, dt_raw, valid, conv_w[i], conv_b[i], dt_bias[i], a_log[i],
                     d_skip[i], ssd_norm_w[i])
        h = h + jnp.concatenate([y_ret, y_ssd], axis=-1) @ w_out[i]
        u = _rmsnorm(h, norm2_w[i]) @ w_ff1[i]
        h = h + jnp.square(jax.nn.relu(u)) @ w_ff2[i]

    return _rmsnorm(h, final_norm_w)[:, CHUNK:]
```

```python
import functools

import jax
import jax.numpy as jnp
from jax import lax
from jax.experimental import pallas as pl
from jax.experimental.pallas import tpu as pltpu

F32 = jnp.float32
BF16 = jnp.bfloat16

D_MODEL = 1024
CHUNK = 64
N_META = 16
PAD = CHUNK - N_META
EPS = 1e-6
ROPE_BASE = 10000.0

RET_HEADS = 8
RET_DK = 128
RET_W = RET_HEADS * RET_DK

SSD_INNER = 1024
SSD_HEAD_DIM = 64
SSD_HEADS = SSD_INNER // SSD_HEAD_DIM
SSD_GROUPS = 4
SSD_HPG = SSD_HEADS // SSD_GROUPS
SSD_STATE = 128
SSD_CONV = 4
SSD_BC = SSD_GROUPS * SSD_STATE
SSD_CONV_DIM = SSD_INNER + 2 * SSD_BC
SSD_GW = SSD_HPG * SSD_HEAD_DIM

MIX_WIDTH = RET_W + SSD_INNER
D_FF = 4 * D_MODEL

C_Q, C_K, C_V, C_G, C_Z, C_XBC, C_DT, C_END = (
    0, 1024, 2048, 3072, 4096, 5120, 7168, 8192)

HIST = 8

MIXER_TILE = 256
FFN_TILE = 512
FFN_SLAB = 1024
VMEM_LIMIT = 56 * 1024 * 1024


def _silu(v):
    return v * jax.nn.sigmoid(v)


def _softplus(v):
    return jnp.maximum(v, 0.0) + jnp.log1p(jnp.exp(-jnp.abs(v)))


def _dot(a, b):
    return jnp.dot(a, b, preferred_element_type=F32)


def _dot_nt(a, b):
    return lax.dot_general(a, b, (((1,), (1,)), ((), ())),
                           preferred_element_type=F32)


def _dot_tn(a, b):
    return lax.dot_general(a, b, (((0,), (0,)), ((), ())),
                           preferred_element_type=F32)


def _mixer_kernel(*refs, tt, masked, has_init, emit_state):
    it = iter(refs)
    x_ref = next(it)
    rope_ref = next(it)
    valid_ref = next(it) if masked else None
    n1_ref = next(it)
    w1_ref = next(it)
    retw_ref = next(it)
    dq_ref = next(it)
    dk_ref = next(it)
    dmask_ref = next(it)
    cdec_ref = next(it)
    convw_ref = next(it)
    convb_ref = next(it)
    dtb_ref = next(it)
    alog_ref = next(it)
    dsk_ref = next(it)
    ssdw_ref = next(it)
    wout_ref = next(it)
    if has_init:
        s0_ref, h0_ref, c0_ref = next(it), next(it), next(it)
    out_ref = next(it)
    if emit_state:
        so_ref, ho_ref, co_ref = next(it), next(it), next(it)
    (qr_s, kr_s, qd_s, kd_s, v_s, gate_s, zs_s, xs_s, dt_s, xbuf, b_s, c_s,
     y_s, sret, hssd) = it

    t = pl.program_id(1)

    @pl.when(t == 0)
    def _init():
        if has_init:
            sret[...] = s0_ref[...]
            hssd[...] = h0_ref[...]
            xbuf[0:HIST, :] = c0_ref[...]
        else:
            sret[...] = jnp.zeros_like(sret)
            hssd[...] = jnp.zeros_like(hssd)
            xbuf[0:HIST, :] = jnp.zeros((HIST, SSD_CONV_DIM), F32)

    x = x_ref[0]
    ms = jnp.mean(x * x, axis=-1, keepdims=True)
    xn = ((x * lax.rsqrt(ms + EPS)) * n1_ref[...]).astype(BF16)

    def proj(lo, hi):
        return _dot(xn, w1_ref[:, lo:hi])

    qc, qs, kc, ks = rope_ref[0], rope_ref[1], rope_ref[2], rope_ref[3]
    qf = proj(C_Q, C_K)
    kf = proj(C_K, C_V)
    for h in range(RET_HEADS):
        sl = slice(RET_DK * h, RET_DK * (h + 1))
        qh = qf[:, sl]
        qr = qh * qc + pltpu.roll(qh, RET_DK // 2, 1) * qs
        qr_s[:, sl] = qr.astype(BF16)
        qd_s[:, sl] = (qr * dq_ref[:, sl]).astype(BF16)
        kh = kf[:, sl]
        kr = kh * kc + pltpu.roll(kh, RET_DK // 2, 1) * ks
        kr_s[:, sl] = kr.astype(BF16)
        kd_s[:, sl] = (kr * dk_ref[:, sl]).astype(BF16)

    v_s[...] = proj(C_V, C_G).astype(BF16)
    gate_s[...] = _silu(proj(C_G, C_Z)) * retw_ref[...]
    zs_s[...] = _silu(proj(C_Z, C_XBC))

    xbuf[HIST:HIST + tt, :] = proj(C_XBC, C_DT)
    conv = convb_ref[...] + convw_ref[SSD_CONV - 1:SSD_CONV, :] * xbuf[HIST:HIST + tt, :]
    for j in range(SSD_CONV - 1):
        off = HIST - (SSD_CONV - 1) + j
        conv = conv + convw_ref[j:j + 1, :] * xbuf[off:off + tt, :]
    xbuf[0:HIST, :] = xbuf[tt:tt + HIST, :]
    act = _silu(conv)
    dt = _softplus(proj(C_DT, C_END) + dtb_ref[...])
    if masked:
        vcol = valid_ref[:, 0:1]
        act = act * vcol
        dt = dt * vcol
    xs_s[...] = act[:, :SSD_INNER]
    b_s[...] = act[:, SSD_INNER:SSD_INNER + SSD_BC].astype(BF16)
    c_s[...] = act[:, SSD_INNER + SSD_BC:].astype(BF16)
    dt_s[...] = dt

    row = lax.broadcasted_iota(jnp.int32, (CHUNK, SSD_INNER), 0)
    col = lax.broadcasted_iota(jnp.int32, (CHUNK, SSD_INNER), 1) & (CHUNK - 1)
    tri_r = lax.broadcasted_iota(jnp.int32, (CHUNK, CHUNK), 0)
    tri_c = lax.broadcasted_iota(jnp.int32, (CHUNK, CHUNK), 1)
    tril = (tri_c <= tri_r).astype(BF16)
    blk_r = lax.broadcasted_iota(jnp.int32, (SSD_GW, SSD_GW), 0) // SSD_HEAD_DIM
    blk_c = lax.broadcasted_iota(jnp.int32, (SSD_GW, SSD_GW), 1) // SSD_HEAD_DIM
    a_neg = -jnp.exp(alog_ref[...])

    def chunk_step(c, carry):
        r0 = pl.multiple_of(c * CHUNK, CHUNK)
        rows = pl.ds(r0, CHUNK)

        for h in range(RET_HEADS):
            sl = slice(RET_DK * h, RET_DK * (h + 1))
            vh = v_s[rows, sl]
            s = _dot_nt(qr_s[rows, sl], kr_s[rows, sl])
            p = (s * dmask_ref[h]).astype(BF16)
            st = sret[h]
            o = _dot(p, vh) + _dot(qd_s[rows, sl], st.astype(BF16))
            sret[h] = cdec_ref[h:h + 1, :] * st + _dot_tn(kd_s[rows, sl], vh)
            o = o * lax.rsqrt(jnp.mean(o * o, axis=-1, keepdims=True) + EPS)
            y_s[rows, sl] = (o * gate_s[rows, sl]).astype(BF16)

        dtc = dt_s[rows, :]
        da = dtc * a_neg
        hi = da.astype(BF16)
        r1 = da - hi.astype(F32)
        mid = r1.astype(BF16)
        low = (r1 - mid.astype(F32)).astype(BF16)
        acol = _dot(tril, hi) + _dot(tril, mid) + _dot(tril, low)
        arow = jnp.sum(jnp.where(row == col, acol, 0.0), axis=0, keepdims=True)
        lmat = jnp.exp(jnp.where(col <= row, acol - arow, -1e30))
        a_last = acol[CHUNK - 1:CHUNK, :]
        exp_a = jnp.exp(acol)
        xsc = xs_s[rows, :]
        xdt = xsc * dtc
        xw = (xdt * jnp.exp(a_last - acol)).astype(BF16)
        xdt_b = xdt.astype(BF16)
        exp_last = jnp.exp(a_last)
        skip = xsc * dsk_ref[...]
        zsc = zs_s[rows, :]
        for g in range(SSD_GROUPS):
            gl = slice(SSD_GW * g, SSD_GW * (g + 1))
            nl = slice(SSD_STATE * g, SSD_STATE * (g + 1))
            cg = c_s[rows, nl]
            bg = b_s[rows, nl]
            brep = jnp.concatenate([bg] * SSD_HPG, axis=0)
            w = (_dot_nt(cg, brep) * lmat[:, gl]).astype(BF16)
            xg = xdt_b[:, gl]
            xrep = jnp.concatenate([xg] * SSD_HPG, axis=0)
            bd = jnp.where(blk_r == blk_c, xrep, jnp.zeros_like(xrep))
            hg = hssd[g]
            yg = (_dot(w, bd) + _dot(cg, hg.astype(BF16)) * exp_a[:, gl]
                  + skip[:, gl])
            hssd[g] = exp_last[:, gl] * hg + _dot_tn(bg, xw[:, gl])
            yg = yg * zsc[:, gl]
            yg = yg * lax.rsqrt(jnp.mean(yg * yg, axis=-1, keepdims=True) + EPS)
            y_s[rows, pl.ds(RET_W + SSD_GW * g, SSD_GW)] = (
                yg * ssdw_ref[:, gl]).astype(BF16)
        return carry

    lax.fori_loop(0, tt // CHUNK, chunk_step, 0)

    out_ref[0] = x + _dot(y_s[...], wout_ref[...])

    if emit_state:
        so_ref[...] = sret[...]
        ho_ref[...] = hssd[...]
        co_ref[...] = xbuf[0:HIST, :]


def _const_spec(shape, single=True):
    nd = len(shape)
    kw = {"pipeline_mode": pl.Buffered(1)} if single else {}
    return pl.BlockSpec(shape, lambda *_: (0,) * nd, **kw)


def _mixer_call(x, rope, valid, consts, init, *, tt, emit_state):
    b, n, _ = x.shape
    masked = valid is not None
    has_init = init is not None
    nt = n // tt

    in_specs = [pl.BlockSpec((1, tt, D_MODEL), lambda i, j: (i, j, 0)),
                pl.BlockSpec((4, tt, RET_DK), lambda i, j: (0, j, 0))]
    args = [x, rope]
    if masked:
        in_specs.append(pl.BlockSpec((tt, 128), lambda i, j: (j, 0)))
        args.append(valid)
    for c in consts:
        in_specs.append(_const_spec(c.shape))
        args.append(c)
    if has_init:
        for c in init:
            in_specs.append(_const_spec(c.shape))
            args.append(c)

    out_shape = [jax.ShapeDtypeStruct((b, n, D_MODEL), F32)]
    out_specs = [pl.BlockSpec((1, tt, D_MODEL), lambda i, j: (i, j, 0))]
    if emit_state:
        st_shapes = [(RET_HEADS, RET_DK, RET_DK), (SSD_GROUPS, SSD_STATE, SSD_GW),
                     (HIST, SSD_CONV_DIM)]
        for s in st_shapes:
            out_shape.append(jax.ShapeDtypeStruct(s, F32))
            out_specs.append(_const_spec(s, single=False))

    scratch = [pltpu.VMEM((tt, RET_W), BF16)] * 5
    scratch += [pltpu.VMEM((tt, SSD_INNER), F32)] * 4
    scratch += [pltpu.VMEM((tt + HIST, SSD_CONV_DIM), F32),
                pltpu.VMEM((tt, SSD_BC), BF16), pltpu.VMEM((tt, SSD_BC), BF16),
                pltpu.VMEM((tt, MIX_WIDTH), BF16),
                pltpu.VMEM((RET_HEADS, RET_DK, RET_DK), F32),
                pltpu.VMEM((SSD_GROUPS, SSD_STATE, SSD_GW), F32)]

    kern = functools.partial(_mixer_kernel, tt=tt, masked=masked,
                             has_init=has_init, emit_state=emit_state)
    return pl.pallas_call(
        kern,
        grid=(b, nt),
        in_specs=in_specs,
        out_specs=out_specs,
        out_shape=out_shape,
        scratch_shapes=scratch,
        compiler_params=pltpu.CompilerParams(
            dimension_semantics=("arbitrary", "arbitrary"),
            vmem_limit_bytes=VMEM_LIMIT),
        name="mixer_meta" if emit_state else "mixer",
    )(*args)


def _ffn_kernel(h_ref, n2_ref, w1_ref, w2_ref, fn_ref, out_ref):
    h = h_ref[...]
    ms = jnp.mean(h * h, axis=-1, keepdims=True)
    u = ((h * lax.rsqrt(ms + EPS)) * n2_ref[...]).astype(BF16)
    acc = h
    for j in range(D_FF // FFN_SLAB):
        sl = slice(FFN_SLAB * j, FFN_SLAB * (j + 1))
        a = jnp.maximum(_dot(u, w1_ref[:, sl]), 0.0)
        acc = acc + _dot((a * a).astype(BF16), w2_ref[sl, :])
    ms2 = jnp.mean(acc * acc, axis=-1, keepdims=True)
    out_ref[...] = (acc * lax.rsqrt(ms2 + EPS)) * fn_ref[...]


def _ffn_call(h, n2, w1, w2, fn, *, tm):
    n = h.shape[0]
    return pl.pallas_call(
        _ffn_kernel,
        grid=(n // tm,),
        in_specs=[pl.BlockSpec((tm, D_MODEL), lambda i: (i, 0)),
                  _const_spec(n2.shape), _const_spec(w1.shape),
                  _const_spec(w2.shape), _const_spec(fn.shape)],
        out_specs=pl.BlockSpec((tm, D_MODEL), lambda i: (i, 0)),
        out_shape=jax.ShapeDtypeStruct((n, D_MODEL), F32),
        compiler_params=pltpu.CompilerParams(
            dimension_semantics=("arbitrary",),
            vmem_limit_bytes=VMEM_LIMIT),
        name="ffn",
    )(h, n2, w1, w2, fn)


def _rope_tables(pos):
    half = RET_DK // 2
    freqs = ROPE_BASE ** (-jnp.arange(0, half, dtype=F32) / half)
    ang = pos[:, None] * freqs[None, :]
    cos, sin = jnp.cos(ang), jnp.sin(ang)
    cos2 = jnp.concatenate([cos, cos], axis=-1)
    sin2 = jnp.concatenate([-sin, sin], axis=-1)
    scale = RET_DK ** -0.5
    return jnp.stack([cos2 * scale, sin2 * scale, cos2, sin2])


def _lane_rep(v, n):
    return jnp.repeat(v, n, axis=-1)


def kernel(x, meta_tokens, norm1_w, w_in, ret_norm_w, conv_w, conv_b, dt_bias,
           a_log, d_skip, ssd_norm_w, w_out, norm2_w, w_ff1, w_ff2, final_norm_w):
    b, seq, d = x.shape
    assert d == D_MODEL and w_in.shape[0] == 1, "single-layer block only"
    tt = min(MIXER_TILE, seq)
    assert seq % tt == 0 and tt % CHUNK == 0

    w = w_in[0]
    w1 = jnp.concatenate(
        [w[:, :C_DT], _lane_rep(w[:, C_DT:], SSD_HEAD_DIM)], axis=1).astype(BF16)
    row2 = lambda v: v.reshape(1, -1).astype(F32)
    hrow = lambda v: _lane_rep(v.reshape(1, -1).astype(F32), SSD_HEAD_DIM)

    log_gamma = jnp.log1p(-jnp.exp2(-5.0 - jnp.arange(RET_HEADS, dtype=F32)))
    idx = jnp.arange(CHUNK, dtype=F32)
    dmask = jnp.exp(log_gamma[:, None, None]
                    * jnp.abs(idx[:, None] - idx[None, :])[None])
    dq = _lane_rep(jnp.exp((idx + 1.0)[:, None] * log_gamma[None, :]), RET_DK)
    dk = _lane_rep(jnp.exp((CHUNK - 1.0 - idx)[:, None] * log_gamma[None, :]), RET_DK)
    cdec = _lane_rep(jnp.exp(CHUNK * log_gamma)[:, None], RET_DK)

    def consts(tile):
        rep = tile // CHUNK
        return [row2(norm1_w[0]), w1, row2(ret_norm_w[0]),
                jnp.tile(dq, (rep, 1)), jnp.tile(dk, (rep, 1)), dmask, cdec,
                conv_w[0].astype(F32), row2(conv_b[0]), hrow(dt_bias[0]),
                hrow(a_log[0]), hrow(d_skip[0]), row2(ssd_norm_w[0]),
                w_out[0].astype(BF16)]

    m_idx = jnp.arange(CHUNK)
    x_meta = jnp.concatenate(
        [jnp.zeros((PAD, D_MODEL), x.dtype), meta_tokens.astype(x.dtype)])[None]
    valid = jnp.broadcast_to((m_idx >= PAD).astype(F32)[:, None], (CHUNK, 128))
    rope_meta = _rope_tables((m_idx - PAD).astype(F32))
    _, s0, h0, c0 = _mixer_call(x_meta, rope_meta, valid, consts(CHUNK), None,
                                tt=CHUNK, emit_state=True)

    rope = _rope_tables((jnp.arange(seq) + N_META).astype(F32))
    (h1,) = _mixer_call(x, rope, None, consts(tt), (s0, h0, c0),
                        tt=tt, emit_state=False)

    tm = min(FFN_TILE, b * seq)
    out = _ffn_call(h1.reshape(b * seq, D_MODEL), row2(norm2_w[0]),
                    w_ff1[0].astype(BF16), w_ff2[0].astype(BF16),
                    row2(final_norm_w), tm=tm)
    return out.reshape(b, seq, D_MODEL)
```

```python
import functools

import jax
import jax.numpy as jnp
from jax import lax
from jax.experimental import pallas as pl
from jax.experimental.pallas import tpu as pltpu

F32 = jnp.float32
BF16 = jnp.bfloat16

D_MODEL = 1024
CHUNK = 64
N_META = 16
PAD = CHUNK - N_META
EPS = 1e-6
ROPE_BASE = 10000.0

RET_HEADS = 8
RET_DK = 128
RET_W = RET_HEADS * RET_DK

SSD_INNER = 1024
SSD_HEAD_DIM = 64
SSD_HEADS = SSD_INNER // SSD_HEAD_DIM
SSD_GROUPS = 4
SSD_HPG = SSD_HEADS // SSD_GROUPS
SSD_STATE = 128
SSD_CONV = 4
SSD_BC = SSD_GROUPS * SSD_STATE
SSD_CONV_DIM = SSD_INNER + 2 * SSD_BC
SSD_GW = SSD_HPG * SSD_HEAD_DIM

MIX_WIDTH = RET_W + SSD_INNER
D_FF = 4 * D_MODEL

C_Q, C_K, C_V, C_G, C_Z, C_XBC, C_DT, C_END = (
    0, 1024, 2048, 3072, 4096, 5120, 7168, 8192)

HIST = 8

MIXER_TILE = 256
FFN_TILE = 512
FFN_SLAB = 1024
VMEM_LIMIT = 56 * 1024 * 1024


def _silu(v):
    return v * jax.nn.sigmoid(v)


def _softplus(v):
    return jnp.maximum(v, 0.0) + jnp.log1p(jnp.exp(-jnp.abs(v)))


def _dot(a, b):
    return jnp.dot(a, b, preferred_element_type=F32)


def _dot_nt(a, b):
    return lax.dot_general(a, b, (((1,), (1,)), ((), ())),
                           preferred_element_type=F32)


def _dot_tn(a, b):
    return lax.dot_general(a, b, (((0,), (0,)), ((), ())),
                           preferred_element_type=F32)


def _mixer_kernel(*refs, tt, masked, has_init, emit_state):
    it = iter(refs)
    x_ref = next(it)
    rope_ref = next(it)
    valid_ref = next(it) if masked else None
    n1_ref = next(it)
    w1_ref = next(it)
    retw_ref = next(it)
    dq_ref = next(it)
    dk_ref = next(it)
    dmask_ref = next(it)
    cdec_ref = next(it)
    convw_ref = next(it)
    convb_ref = next(it)
    dtb_ref = next(it)
    alog_ref = next(it)
    dsk_ref = next(it)
    ssdw_ref = next(it)
    wout_ref = next(it)
    if has_init:
        s0_ref, h0_ref, c0_ref = next(it), next(it), next(it)
    out_ref = next(it)
    if emit_state:
        so_ref, ho_ref, co_ref = next(it), next(it), next(it)
    (qr_s, kr_s, qd_s, kd_s, v_s, gate_s, zs_s, xs_s, dt_s, xbuf, b_s, c_s,
     y_s, sret, hssd) = it

    t = pl.program_id(1)

    @pl.when(t == 0)
    def _init():
        if has_init:
            sret[...] = s0_ref[...]
            hssd[...] = h0_ref[...]
            xbuf[0:HIST, :] = c0_ref[...]
        else:
            sret[...] = jnp.zeros_like(sret)
            hssd[...] = jnp.zeros_like(hssd)
            xbuf[0:HIST, :] = jnp.zeros((HIST, SSD_CONV_DIM), F32)

    x = x_ref[0]
    ms = jnp.mean(x * x, axis=-1, keepdims=True)
    xn = ((x * lax.rsqrt(ms + EPS)) * n1_ref[...]).astype(BF16)

    def proj(lo, hi):
        return _dot(xn, w1_ref[:, lo:hi])

    qc, qs, kc, ks = rope_ref[0], rope_ref[1], rope_ref[2], rope_ref[3]
    qf = proj(C_Q, C_K)
    kf = proj(C_K, C_V)
    for h in range(RET_HEADS):
        sl = slice(RET_DK * h, RET_DK * (h + 1))
        qh = qf[:, sl]
        qr = qh * qc + pltpu.roll(qh, RET_DK // 2, 1) * qs
        qr_s[:, sl] = qr.astype(BF16)
        qd_s[:, sl] = (qr * dq_ref[:, sl]).astype(BF16)
        kh = kf[:, sl]
        kr = kh * kc + pltpu.roll(kh, RET_DK // 2, 1) * ks
        kr_s[:, sl] = kr.astype(BF16)
        kd_s[:, sl] = (kr * dk_ref[:, sl]).astype(BF16)

    v_s[...] = proj(C_V, C_G).astype(BF16)
    gate_s[...] = _silu(proj(C_G, C_Z)) * retw_ref[...]
    zs_s[...] = _silu(proj(C_Z, C_XBC))

    xbuf[HIST:HIST + tt, :] = proj(C_XBC, C_DT)
    conv = convb_ref[...] + convw_ref[SSD_CONV - 1:SSD_CONV, :] * xbuf[HIST:HIST + tt, :]
    for j in range(SSD_CONV - 1):
        off = HIST - (SSD_CONV - 1) + j
        conv = conv + convw_ref[j:j + 1, :] * xbuf[off:off + tt, :]
    xbuf[0:HIST, :] = xbuf[tt:tt + HIST, :]
    act = _silu(conv)
    dt = _softplus(proj(C_DT, C_END) + dtb_ref[...])
    if masked:
        vcol = valid_ref[:, 0:1]
        act = act * vcol
        dt = dt * vcol
    xs_s[...] = act[:, :SSD_INNER]
    b_s[...] = act[:, SSD_INNER:SSD_INNER + SSD_BC].astype(BF16)
    c_s[...] = act[:, SSD_INNER + SSD_BC:].astype(BF16)
    dt_s[...] = dt

    row = lax.broadcasted_iota(jnp.int32, (CHUNK, SSD_INNER), 0)
    col = lax.broadcasted_iota(jnp.int32, (CHUNK, SSD_INNER), 1) & (CHUNK - 1)
    tri_r = lax.broadcasted_iota(jnp.int32, (CHUNK, CHUNK), 0)
    tri_c = lax.broadcasted_iota(jnp.int32, (CHUNK, CHUNK), 1)
    tril = (tri_c <= tri_r).astype(BF16)
    blk_r = lax.broadcasted_iota(jnp.int32, (SSD_GW, SSD_GW), 0) // SSD_HEAD_DIM
    blk_c = lax.broadcasted_iota(jnp.int32, (SSD_GW, SSD_GW), 1) // SSD_HEAD_DIM
    a_neg = -jnp.exp(alog_ref[...])

    def ret_step(c):
        rows = slice(c * CHUNK, (c + 1) * CHUNK)
        for h in range(RET_HEADS):
            sl = slice(RET_DK * h, RET_DK * (h + 1))
            vh = v_s[rows, sl]
            s = _dot_nt(qr_s[rows, sl], kr_s[rows, sl])
            p = (s * dmask_ref[h]).astype(BF16)
            st = sret[h]
            o = _dot(p, vh) + _dot(qd_s[rows, sl], st.astype(BF16))
            sret[h] = cdec_ref[h:h + 1, :] * st + _dot_tn(kd_s[rows, sl], vh)
            o = o * lax.rsqrt(jnp.mean(o * o, axis=-1, keepdims=True) + EPS)
            y_s[rows, sl] = (o * gate_s[rows, sl]).astype(BF16)

    def ssd_step(c):
        rows = slice(c * CHUNK, (c + 1) * CHUNK)
        dtc = dt_s[rows, :]
        da = dtc * a_neg
        hi = da.astype(BF16)
        mid = (da - hi.astype(F32)).astype(BF16)
        acol = _dot(tril, hi) + _dot(tril, mid)
        arow = jnp.sum(jnp.where(row == col, acol, 0.0), axis=0, keepdims=True)
        lmat = jnp.exp(jnp.where(col <= row, acol - arow, -1e30))
        a_last = acol[CHUNK - 1:CHUNK, :]
        exp_a = jnp.exp(acol)
        xsc = xs_s[rows, :]
        xdt = xsc * dtc
        xw = (xdt * jnp.exp(a_last - acol)).astype(BF16)
        xdt_b = xdt.astype(BF16)
        exp_last = jnp.exp(a_last)
        skip = xsc * dsk_ref[...]
        zsc = zs_s[rows, :]
        for g in range(SSD_GROUPS):
            gl = slice(SSD_GW * g, SSD_GW * (g + 1))
            nl = slice(SSD_STATE * g, SSD_STATE * (g + 1))
            cg = c_s[rows, nl]
            bg = b_s[rows, nl]
            brep = jnp.concatenate([bg] * SSD_HPG, axis=0)
            w = (_dot_nt(cg, brep) * lmat[:, gl]).astype(BF16)
            xg = xdt_b[:, gl]
            xrep = jnp.concatenate([xg] * SSD_HPG, axis=0)
            bd = jnp.where(blk_r == blk_c, xrep, jnp.zeros_like(xrep))
            hg = hssd[g]
            yg = (_dot(w, bd) + _dot(cg, hg.astype(BF16)) * exp_a[:, gl]
                  + skip[:, gl])
            hssd[g] = exp_last[:, gl] * hg + _dot_tn(bg, xw[:, gl])
            yg = yg * zsc[:, gl]
            yg = yg * lax.rsqrt(jnp.mean(yg * yg, axis=-1, keepdims=True) + EPS)
            y_s[rows, RET_W + SSD_GW * g:RET_W + SSD_GW * (g + 1)] = (
                yg * ssdw_ref[:, gl]).astype(BF16)

    for c in range(tt // CHUNK):
        ret_step(c)
    for c in range(tt // CHUNK):
        ssd_step(c)

    out_ref[0] = x + _dot(y_s[...], wout_ref[...])

    if emit_state:
        so_ref[...] = sret[...]
        ho_ref[...] = hssd[...]
        co_ref[...] = xbuf[0:HIST, :]


def _const_spec(shape, single=True):
    nd = len(shape)
    kw = {"pipeline_mode": pl.Buffered(1)} if single else {}
    return pl.BlockSpec(shape, lambda *_: (0,) * nd, **kw)


def _mixer_call(x, rope, valid, consts, init, *, tt, emit_state):
    b, n, _ = x.shape
    masked = valid is not None
    has_init = init is not None
    nt = n // tt

    in_specs = [pl.BlockSpec((1, tt, D_MODEL), lambda i, j: (i, j, 0)),
                pl.BlockSpec((4, tt, RET_DK), lambda i, j: (0, j, 0))]
    args = [x, rope]
    if masked:
        in_specs.append(pl.BlockSpec((tt, 128), lambda i, j: (j, 0)))
        args.append(valid)
    for c in consts:
        in_specs.append(_const_spec(c.shape))
        args.append(c)
    if has_init:
        for c in init:
            in_specs.append(_const_spec(c.shape))
            args.append(c)

    out_shape = [jax.ShapeDtypeStruct((b, n, D_MODEL), F32)]
    out_specs = [pl.BlockSpec((1, tt, D_MODEL), lambda i, j: (i, j, 0))]
    if emit_state:
        st_shapes = [(RET_HEADS, RET_DK, RET_DK), (SSD_GROUPS, SSD_STATE, SSD_GW),
                     (HIST, SSD_CONV_DIM)]
        for s in st_shapes:
            out_shape.append(jax.ShapeDtypeStruct(s, F32))
            out_specs.append(_const_spec(s, single=False))

    scratch = [pltpu.VMEM((tt, RET_W), BF16)] * 5
    scratch += [pltpu.VMEM((tt, SSD_INNER), F32)] * 4
    scratch += [pltpu.VMEM((tt + HIST, SSD_CONV_DIM), F32),
                pltpu.VMEM((tt, SSD_BC), BF16), pltpu.VMEM((tt, SSD_BC), BF16),
                pltpu.VMEM((tt, MIX_WIDTH), BF16),
                pltpu.VMEM((RET_HEADS, RET_DK, RET_DK), F32),
                pltpu.VMEM((SSD_GROUPS, SSD_STATE, SSD_GW), F32)]

    kern = functools.partial(_mixer_kernel, tt=tt, masked=masked,
                             has_init=has_init, emit_state=emit_state)
    return pl.pallas_call(
        kern,
        grid=(b, nt),
        in_specs=in_specs,
        out_specs=out_specs,
        out_shape=out_shape,
        scratch_shapes=scratch,
        compiler_params=pltpu.CompilerParams(
            dimension_semantics=("arbitrary", "arbitrary"),
            vmem_limit_bytes=VMEM_LIMIT),
        name="mixer_meta" if emit_state else "mixer",
    )(*args)


def _ffn_kernel(h_ref, n2_ref, w1_ref, w2_ref, fn_ref, out_ref):
    h = h_ref[...]
    ms = jnp.mean(h * h, axis=-1, keepdims=True)
    u = ((h * lax.rsqrt(ms + EPS)) * n2_ref[...]).astype(BF16)
    acc = h
    for j in range(D_FF // FFN_SLAB):
        sl = slice(FFN_SLAB * j, FFN_SLAB * (j + 1))
        a = jnp.maximum(_dot(u, w1_ref[:, sl]), 0.0)
        acc = acc + _dot((a * a).astype(BF16), w2_ref[sl, :])
    ms2 = jnp.mean(acc * acc, axis=-1, keepdims=True)
    out_ref[...] = (acc * lax.rsqrt(ms2 + EPS)) * fn_ref[...]


def _ffn_call(h, n2, w1, w2, fn, *, tm):
    n = h.shape[0]
    return pl.pallas_call(
        _ffn_kernel,
        grid=(n // tm,),
        in_specs=[pl.BlockSpec((tm, D_MODEL), lambda i: (i, 0)),
                  _const_spec(n2.shape), _const_spec(w1.shape),
                  _const_spec(w2.shape), _const_spec(fn.shape)],
        out_specs=pl.BlockSpec((tm, D_MODEL), lambda i: (i, 0)),
        out_shape=jax.ShapeDtypeStruct((n, D_MODEL), F32),
        compiler_params=pltpu.CompilerParams(
            dimension_semantics=("arbitrary",),
            vmem_limit_bytes=VMEM_LIMIT),
        name="ffn",
    )(h, n2, w1, w2, fn)


def _rope_tables(pos):
    half = RET_DK // 2
    freqs = ROPE_BASE ** (-jnp.arange(0, half, dtype=F32) / half)
    ang = pos[:, None] * freqs[None, :]
    cos, sin = jnp.cos(ang), jnp.sin(ang)
    cos2 = jnp.concatenate([cos, cos], axis=-1)
    sin2 = jnp.concatenate([-sin, sin], axis=-1)
    scale = RET_DK ** -0.5
    return jnp.stack([cos2 * scale, sin2 * scale, cos2, sin2])


def _lane_rep(v, n):
    return jnp.repeat(v, n, axis=-1)


def kernel(x, meta_tokens, norm1_w, w_in, ret_norm_w, conv_w, conv_b, dt_bias,
           a_log, d_skip, ssd_norm_w, w_out, norm2_w, w_ff1, w_ff2, final_norm_w):
    b, seq, d = x.shape
    assert d == D_MODEL and w_in.shape[0] == 1, "single-layer block only"
    tt = min(MIXER_TILE, seq)
    assert seq % tt == 0 and tt % CHUNK == 0

    w = w_in[0]
    w1 = jnp.concatenate(
        [w[:, :C_DT], _lane_rep(w[:, C_DT:], SSD_HEAD_DIM)], axis=1).astype(BF16)
    row2 = lambda v: v.reshape(1, -1).astype(F32)
    hrow = lambda v: _lane_rep(v.reshape(1, -1).astype(F32), SSD_HEAD_DIM)

    log_gamma = jnp.log1p(-jnp.exp2(-5.0 - jnp.arange(RET_HEADS, dtype=F32)))
    idx = jnp.arange(CHUNK, dtype=F32)
    dmask = jnp.exp(log_gamma[:, None, None]
                    * jnp.abs(idx[:, None] - idx[None, :])[None])
    dq = _lane_rep(jnp.exp((idx + 1.0)[:, None] * log_gamma[None, :]), RET_DK)
    dk = _lane_rep(jnp.exp((CHUNK - 1.0 - idx)[:, None] * log_gamma[None, :]), RET_DK)
    cdec = _lane_rep(jnp.exp(CHUNK * log_gamma)[:, None], RET_DK)

    def consts(tile):
        rep = tile // CHUNK
        return [row2(norm1_w[0]), w1, row2(ret_norm_w[0]),
                jnp.tile(dq, (rep, 1)), jnp.tile(dk, (rep, 1)), dmask, cdec,
                conv_w[0].astype(F32), row2(conv_b[0]), hrow(dt_bias[0]),
                hrow(a_log[0]), hrow(d_skip[0]), row2(ssd_norm_w[0]),
                w_out[0].astype(BF16)]

    m_idx = jnp.arange(CHUNK)
    x_meta = jnp.concatenate(
        [jnp.zeros((PAD, D_MODEL), x.dtype), meta_tokens.astype(x.dtype)])[None]
    valid = jnp.broadcast_to((m_idx >= PAD).astype(F32)[:, None], (CHUNK, 128))
    rope_meta = _rope_tables((m_idx - PAD).astype(F32))
    _, s0, h0, c0 = _mixer_call(x_meta, rope_meta, valid, consts(CHUNK), None,
                                tt=CHUNK, emit_state=True)

    rope = _rope_tables((jnp.arange(seq) + N_META).astype(F32))
    (h1,) = _mixer_call(x, rope, None, consts(tt), (s0, h0, c0),
                        tt=tt, emit_state=False)

    tm = min(FFN_TILE, b * seq)
    out = _ffn_call(h1.reshape(b * seq, D_MODEL), row2(norm2_w[0]),
                    w_ff1[0].astype(BF16), w_ff2[0].astype(BF16),
                    row2(final_norm_w), tm=tm)
    return out.reshape(b, seq, D_MODEL)
```

```python
import functools

import jax
import jax.numpy as jnp
from jax import lax
from jax.experimental import pallas as pl
from jax.experimental.pallas import tpu as pltpu

F32 = jnp.float32
BF16 = jnp.bfloat16

D_MODEL = 1024
CHUNK = 64
N_META = 16
PAD = CHUNK - N_META
EPS = 1e-6
ROPE_BASE = 10000.0

RET_HEADS = 8
RET_DK = 128
RET_W = RET_HEADS * RET_DK

SSD_INNER = 1024
SSD_HEAD_DIM = 64
SSD_HEADS = SSD_INNER // SSD_HEAD_DIM
SSD_GROUPS = 4
SSD_HPG = SSD_HEADS // SSD_GROUPS
SSD_STATE = 128
SSD_CONV = 4
SSD_BC = SSD_GROUPS * SSD_STATE
SSD_CONV_DIM = SSD_INNER + 2 * SSD_BC
SSD_GW = SSD_HPG * SSD_HEAD_DIM

MIX_WIDTH = RET_W + SSD_INNER
D_FF = 4 * D_MODEL

C_Q, C_K, C_V, C_G, C_Z, C_XBC, C_DT = 0, 1024, 2048, 3072, 4096, 5120, 7168

HIST = 8

MIXER_TILE = 256
FFN_TILE = 512
FFN_SLAB = 1024
VMEM_LIMIT = 56 * 1024 * 1024


def _silu(v):
    return v * jax.nn.sigmoid(v)


def _softplus(v):
    return jnp.maximum(v, 0.0) + jnp.log(1.0 + jnp.exp(-jnp.abs(v)))


def _dot(a, b):
    return jnp.dot(a, b, preferred_element_type=F32)


def _dot_nt(a, b):
    return lax.dot_general(a, b, (((1,), (1,)), ((), ())),
                           preferred_element_type=F32)


def _dot_tn(a, b):
    return lax.dot_general(a, b, (((0,), (0,)), ((), ())),
                           preferred_element_type=F32)


def _mixer_kernel(*refs, tt, masked, has_init, emit_state):
    it = iter(refs)
    x_ref = next(it)
    rope_ref = next(it)
    valid_ref = next(it) if masked else None
    n1_ref = next(it)
    wm_ref = next(it)
    wdt_ref = next(it)
    retw_ref = next(it)
    dq_ref = next(it)
    dk_ref = next(it)
    dmask_ref = next(it)
    cdec_ref = next(it)
    convw_ref = next(it)
    convb_ref = next(it)
    dtb_ref = next(it)
    alog_ref = next(it)
    dsk_ref = next(it)
    ssdw_ref = next(it)
    wout_ref = next(it)
    if has_init:
        s0_ref, h0_ref, c0_ref = next(it), next(it), next(it)
    out_ref = next(it)
    if emit_state:
        so_ref, ho_ref, co_ref = next(it), next(it), next(it)
    (qr_s, kr_s, qd_s, kd_s, v_s, gate_s, zs_s, xs_s, dt_s, b_s, c_s,
     xbuf, y_s, sret, hssd) = it

    t = pl.program_id(1)

    @pl.when(t == 0)
    def _init():
        if has_init:
            sret[...] = s0_ref[...]
            hssd[...] = h0_ref[...]
            xbuf[0:HIST, :] = c0_ref[...]
        else:
            sret[...] = jnp.zeros_like(sret)
            hssd[...] = jnp.zeros_like(hssd)
            xbuf[0:HIST, :] = jnp.zeros((HIST, SSD_CONV_DIM), F32)

    n_chunks = tt // CHUNK
    chunks = [slice(c * CHUNK, (c + 1) * CHUNK) for c in range(n_chunks)]
    heads = [slice(RET_DK * h, RET_DK * (h + 1)) for h in range(RET_HEADS)]
    glanes = [slice(SSD_GW * g, SSD_GW * (g + 1)) for g in range(SSD_GROUPS)]
    nlanes = [slice(SSD_STATE * g, SSD_STATE * (g + 1)) for g in range(SSD_GROUPS)]

    x = x_ref[0]
    ms = jnp.mean(x * x, axis=-1, keepdims=True)
    xn = ((x * lax.rsqrt(ms + EPS)) * n1_ref[...]).astype(BF16)

    row = lax.broadcasted_iota(jnp.int32, (CHUNK, SSD_INNER), 0)
    col = lax.broadcasted_iota(jnp.int32, (CHUNK, SSD_INNER), 1) & (CHUNK - 1)
    tri_r = lax.broadcasted_iota(jnp.int32, (tt, tt), 0)
    tri_c = lax.broadcasted_iota(jnp.int32, (tt, tt), 1)
    tril = jnp.logical_and(tri_c <= tri_r,
                           tri_c // CHUNK == tri_r // CHUNK).astype(BF16)
    blk_r = lax.broadcasted_iota(jnp.int32, (SSD_GW, SSD_GW), 0) // SSD_HEAD_DIM
    blk_c = lax.broadcasted_iota(jnp.int32, (SSD_GW, SSD_GW), 1) // SSD_HEAD_DIM
    a_neg = -jnp.exp(alog_ref[...])

    dt = _softplus(_dot(xn, wdt_ref[...]) + dtb_ref[...])
    if masked:
        vcol = valid_ref[:, 0:1]
        dt = dt * vcol
    dt_s[...] = dt

    def conv_silu(lo, hi):
        xbuf[HIST:HIST + tt, lo:hi] = _dot(xn, wm_ref[:, C_XBC + lo:C_XBC + hi])
        conv = (convb_ref[:, lo:hi]
                + convw_ref[SSD_CONV - 1:SSD_CONV, lo:hi] * xbuf[HIST:HIST + tt, lo:hi])
        for j in range(SSD_CONV - 1):
            off = HIST - (SSD_CONV - 1) + j
            conv = conv + convw_ref[j:j + 1, lo:hi] * xbuf[off:off + tt, lo:hi]
        xbuf[0:HIST, lo:hi] = xbuf[tt:tt + HIST, lo:hi]
        act = _silu(conv)
        if masked:
            act = act * vcol
        return act

    xs_s[...] = conv_silu(0, SSD_INNER)
    bc = conv_silu(SSD_INNER, SSD_CONV_DIM)
    b_s[...] = bc[:, :SSD_BC].astype(BF16)
    c_s[...] = bc[:, SSD_BC:].astype(BF16)

    qc, qs, kc, ks = rope_ref[0], rope_ref[1], rope_ref[2], rope_ref[3]
    dq_t = jnp.concatenate([dq_ref[...]] * n_chunks, axis=0)
    dk_t = jnp.concatenate([dk_ref[...]] * n_chunks, axis=0)
    qf = _dot(xn, wm_ref[:, C_Q:C_K])
    kf = _dot(xn, wm_ref[:, C_K:C_V])
    for sl in heads:
        qh = qf[:, sl]
        qr = qh * qc + pltpu.roll(qh, RET_DK // 2, 1) * qs
        qr_s[:, sl] = qr.astype(BF16)
        qd_s[:, sl] = (qr * dq_t[:, sl]).astype(BF16)
        kh = kf[:, sl]
        kr = kh * kc + pltpu.roll(kh, RET_DK // 2, 1) * ks
        kr_s[:, sl] = kr.astype(BF16)
        kd_s[:, sl] = (kr * dk_t[:, sl]).astype(BF16)

    da = dt_s[...] * a_neg
    hi = da.astype(BF16)
    mid = (da - hi.astype(F32)).astype(BF16)
    acol_all = _dot(tril, hi) + _dot(tril, mid)
    v_s[...] = _dot(xn, wm_ref[:, C_V:C_G]).astype(BF16)
    cb = [[None] * SSD_GROUPS for _ in chunks]
    for c, rows in enumerate(chunks):
        for g in range(SSD_GROUPS):
            brep = jnp.concatenate([b_s[rows, nlanes[g]]] * SSD_HPG, axis=0)
            cb[c][g] = _dot_nt(c_s[rows, nlanes[g]], brep)

    gate_s[...] = _silu(_dot(xn, wm_ref[:, C_G:C_Z])) * retw_ref[...]

    lhs_o = [[None] * RET_HEADS for _ in chunks]
    kv = [[None] * RET_HEADS for _ in chunks]
    for c, rows in enumerate(chunks):
        for h in range(RET_HEADS):
            sl = heads[h]
            sc = _dot_nt(qr_s[rows, sl], kr_s[rows, sl])
            p = (sc * dmask_ref[h]).astype(BF16)
            lhs_o[c][h] = jnp.concatenate([qd_s[rows, sl], p], axis=1)
            kv[c][h] = _dot_tn(kd_s[rows, sl], v_s[rows, sl])

    zs_s[...] = _silu(_dot(xn, wm_ref[:, C_Z:C_XBC]))

    ydiag = [[None] * SSD_GROUPS for _ in chunks]
    dstate = [[None] * SSD_GROUPS for _ in chunks]
    exp_a, exp_last = [], []
    for c, rows in enumerate(chunks):
        acol = acol_all[rows]
        arow = jnp.sum(jnp.where(row == col, acol, 0.0), axis=0, keepdims=True)
        lmat = jnp.exp(jnp.where(col <= row, acol - arow, -1e30))
        a_last = acol[CHUNK - 1:CHUNK, :]
        exp_a.append(jnp.exp(acol))
        exp_last.append(jnp.exp(a_last))
        xdt = xs_s[rows, :] * dt_s[rows, :]
        xw = (xdt * jnp.exp(a_last - acol)).astype(BF16)
        xdt_b = xdt.astype(BF16)
        for g in range(SSD_GROUPS):
            wgt = (cb[c][g] * lmat[:, glanes[g]]).astype(BF16)
            xrep = jnp.concatenate([xdt_b[:, glanes[g]]] * SSD_HPG, axis=0)
            bd = jnp.where(blk_r == blk_c, xrep, jnp.zeros_like(xrep))
            ydiag[c][g] = _dot(wgt, bd)
            dstate[c][g] = _dot_tn(b_s[rows, nlanes[g]], xw[:, glanes[g]])

    st_in = [[None] * RET_HEADS for _ in chunks]
    for h in range(RET_HEADS):
        st = sret[h]
        for c in range(n_chunks):
            st_in[c][h] = st.astype(BF16)
            st = cdec_ref[h:h + 1, :] * st + kv[c][h]
        sret[h] = st
    hg_in = [[None] * SSD_GROUPS for _ in chunks]
    for g in range(SSD_GROUPS):
        hg = hssd[g]
        for c in range(n_chunks):
            hg_in[c][g] = hg.astype(BF16)
            hg = exp_last[c][:, glanes[g]] * hg + dstate[c][g]
        hssd[g] = hg

    for c, rows in enumerate(chunks):
        for h in range(RET_HEADS):
            o = _dot(lhs_o[c][h],
                     jnp.concatenate([st_in[c][h], v_s[rows, heads[h]]], axis=0))
            o = o * lax.rsqrt(jnp.mean(o * o, axis=-1, keepdims=True) + EPS)
            y_s[rows, heads[h]] = (o * gate_s[rows, heads[h]]).astype(BF16)
    acc = x + _dot(y_s[:, :RET_W], wout_ref[:RET_W, :])
    for c, rows in enumerate(chunks):
        for g in range(SSD_GROUPS):
            gl = glanes[g]
            yg = (ydiag[c][g]
                  + _dot(c_s[rows, nlanes[g]], hg_in[c][g]) * exp_a[c][:, gl]
                  + xs_s[rows, gl] * dsk_ref[:, gl])
            yg = yg * zs_s[rows, gl]
            yg = yg * lax.rsqrt(jnp.mean(yg * yg, axis=-1, keepdims=True) + EPS)
            y_s[rows, RET_W + SSD_GW * g:RET_W + SSD_GW * (g + 1)] = (
                yg * ssdw_ref[:, gl]).astype(BF16)

    out_ref[0] = acc + _dot(y_s[:, RET_W:], wout_ref[RET_W:, :])

    if emit_state:
        so_ref[...] = sret[...]
        ho_ref[...] = hssd[...]
        co_ref[...] = xbuf[0:HIST, :]


def _const_spec(shape, single=True):
    nd = len(shape)
    kw = {"pipeline_mode": pl.Buffered(1)} if single else {}
    return pl.BlockSpec(shape, lambda *_: (0,) * nd, **kw)


def _mixer_call(x, rope, valid, consts, init, *, tt, emit_state):
    b, n, _ = x.shape
    masked = valid is not None
    has_init = init is not None
    nt = n // tt

    in_specs = [pl.BlockSpec((1, tt, D_MODEL), lambda i, j: (i, j, 0)),
                pl.BlockSpec((4, tt, RET_DK), lambda i, j: (0, j, 0))]
    args = [x, rope]
    if masked:
        in_specs.append(pl.BlockSpec((tt, 128), lambda i, j: (j, 0)))
        args.append(valid)
    for c in consts:
        in_specs.append(_const_spec(c.shape))
        args.append(c)
    if has_init:
        for c in init:
            in_specs.append(_const_spec(c.shape))
            args.append(c)

    out_shape = [jax.ShapeDtypeStruct((b, n, D_MODEL), F32)]
    out_specs = [pl.BlockSpec((1, tt, D_MODEL), lambda i, j: (i, j, 0))]
    if emit_state:
        st_shapes = [(RET_HEADS, RET_DK, RET_DK), (SSD_GROUPS, SSD_STATE, SSD_GW),
                     (HIST, SSD_CONV_DIM)]
        for s in st_shapes:
            out_shape.append(jax.ShapeDtypeStruct(s, F32))
            out_specs.append(_const_spec(s, single=False))

    scratch = [pltpu.VMEM((tt, RET_W), BF16)] * 5
    scratch += [pltpu.VMEM((tt, SSD_INNER), F32)] * 4
    scratch += [pltpu.VMEM((tt, SSD_BC), BF16)] * 2
    scratch += [pltpu.VMEM((tt + HIST, SSD_CONV_DIM), F32),
                pltpu.VMEM((tt, MIX_WIDTH), BF16),
                pltpu.VMEM((RET_HEADS, RET_DK, RET_DK), F32),
                pltpu.VMEM((SSD_GROUPS, SSD_STATE, SSD_GW), F32)]

    kern = functools.partial(_mixer_kernel, tt=tt, masked=masked,
                             has_init=has_init, emit_state=emit_state)
    return pl.pallas_call(
        kern,
        grid=(b, nt),
        in_specs=in_specs,
        out_specs=out_specs,
        out_shape=out_shape,
        scratch_shapes=scratch,
        compiler_params=pltpu.CompilerParams(
            dimension_semantics=("arbitrary", "arbitrary"),
            vmem_limit_bytes=VMEM_LIMIT),
        name="mixer_meta" if emit_state else "mixer",
    )(*args)


def _ffn_kernel(h_ref, n2_ref, w1_ref, w2_ref, fn_ref, out_ref):
    h = h_ref[...]
    ms = jnp.mean(h * h, axis=-1, keepdims=True)
    u = ((h * lax.rsqrt(ms + EPS)) * n2_ref[...]).astype(BF16)
    acc = h
    for j in range(D_FF // FFN_SLAB):
        sl = slice(FFN_SLAB * j, FFN_SLAB * (j + 1))
        a = jnp.maximum(_dot(u, w1_ref[:, sl]), 0.0)
        acc = acc + _dot((a * a).astype(BF16), w2_ref[sl, :])
    ms2 = jnp.mean(acc * acc, axis=-1, keepdims=True)
    out_ref[...] = (acc * lax.rsqrt(ms2 + EPS)) * fn_ref[...]


def _ffn_call(h, n2, w1, w2, fn, *, tm):
    n = h.shape[0]
    return pl.pallas_call(
        _ffn_kernel,
        grid=(n // tm,),
        in_specs=[pl.BlockSpec((tm, D_MODEL), lambda i: (i, 0)),
                  _const_spec(n2.shape), _const_spec(w1.shape),
                  _const_spec(w2.shape), _const_spec(fn.shape)],
        out_specs=pl.BlockSpec((tm, D_MODEL), lambda i: (i, 0)),
        out_shape=jax.ShapeDtypeStruct((n, D_MODEL), F32),
        compiler_params=pltpu.CompilerParams(
            dimension_semantics=("arbitrary",),
            vmem_limit_bytes=VMEM_LIMIT),
        name="ffn",
    )(h, n2, w1, w2, fn)


def _rope_tables(pos):
    half = RET_DK // 2
    freqs = ROPE_BASE ** (-jnp.arange(0, half, dtype=F32) / half)
    ang = pos[:, None] * freqs[None, :]
    cos, sin = jnp.cos(ang), jnp.sin(ang)
    cos2 = jnp.concatenate([cos, cos], axis=-1)
    sin2 = jnp.concatenate([-sin, sin], axis=-1)
    scale = RET_DK ** -0.5
    return jnp.stack([cos2 * scale, sin2 * scale, cos2, sin2])


def _lane_rep(v, n):
    return jnp.broadcast_to(v[..., None], v.shape + (n,)).reshape(
        v.shape[:-1] + (v.shape[-1] * n,))


def kernel(x, meta_tokens, norm1_w, w_in, ret_norm_w, conv_w, conv_b, dt_bias,
           a_log, d_skip, ssd_norm_w, w_out, norm2_w, w_ff1, w_ff2, final_norm_w):
    b, seq, d = x.shape
    assert d == D_MODEL and w_in.shape[0] == 1, "single-layer block only"
    tt = min(MIXER_TILE, seq)
    assert seq % tt == 0 and tt % CHUNK == 0

    w = w_in[0]
    w_main = w[:, :C_DT].astype(BF16)
    w_dt = _lane_rep(w[:, C_DT:], SSD_HEAD_DIM).astype(BF16)
    row2 = lambda v: v.reshape(1, -1).astype(F32)
    hrow = lambda v: _lane_rep(v.reshape(1, -1).astype(F32), SSD_HEAD_DIM)

    log_gamma = jnp.log1p(-jnp.exp2(-5.0 - jnp.arange(RET_HEADS, dtype=F32)))
    idx = jnp.arange(CHUNK, dtype=F32)
    dmask = jnp.exp(log_gamma[:, None, None]
                    * jnp.abs(idx[:, None] - idx[None, :])[None])
    dq = _lane_rep(jnp.exp((idx + 1.0)[:, None] * log_gamma[None, :]), RET_DK)
    dk = _lane_rep(jnp.exp((CHUNK - 1.0 - idx)[:, None] * log_gamma[None, :]), RET_DK)
    cdec = _lane_rep(jnp.exp(CHUNK * log_gamma)[:, None], RET_DK)

    consts = [row2(norm1_w[0]), w_main, w_dt, row2(ret_norm_w[0]), dq, dk, dmask,
              cdec, conv_w[0].astype(F32), row2(conv_b[0]), hrow(dt_bias[0]),
              hrow(a_log[0]), hrow(d_skip[0]), row2(ssd_norm_w[0]),
              w_out[0].astype(BF16)]

    m_idx = jnp.arange(CHUNK)
    x_meta = jnp.concatenate(
        [jnp.zeros((PAD, D_MODEL), x.dtype), meta_tokens.astype(x.dtype)])[None]
    valid = jnp.broadcast_to((m_idx >= PAD).astype(F32)[:, None], (CHUNK, 128))
    rope_meta = _rope_tables((m_idx - PAD).astype(F32))
    _, s0, h0, c0 = _mixer_call(x_meta, rope_meta, valid, consts, None,
                                tt=CHUNK, emit_state=True)

    rope = _rope_tables((jnp.arange(seq) + N_META).astype(F32))
    (h1,) = _mixer_call(x, rope, None, consts, (s0, h0, c0),
                        tt=tt, emit_state=False)

    tm = min(FFN_TILE, b * seq)
    out = _ffn_call(h1.reshape(b * seq, D_MODEL), row2(norm2_w[0]),
                    w_ff1[0].astype(BF16), w_ff2[0].astype(BF16),
                    row2(final_norm_w), tm=tm)
    return out.reshape(b, seq, D_MODEL)
```

```python
import functools

import jax
import jax.numpy as jnp
import numpy as np
from jax import lax
from jax.experimental import pallas as pl
from jax.experimental.pallas import tpu as pltpu

F32 = jnp.float32
BF16 = jnp.bfloat16

D_MODEL = 1024
CHUNK = 64
N_META = 16
PAD = CHUNK - N_META
EPS = 1e-6
ROPE_BASE = 10000.0

RET_HEADS = 8
RET_DK = 128
RET_W = RET_HEADS * RET_DK

SSD_INNER = 1024
SSD_HEAD_DIM = 64
SSD_HEADS = SSD_INNER // SSD_HEAD_DIM
SSD_GROUPS = 4
SSD_HPG = SSD_HEADS // SSD_GROUPS
SSD_STATE = 128
SSD_CONV = 4
SSD_BC = SSD_GROUPS * SSD_STATE
SSD_CONV_DIM = SSD_INNER + 2 * SSD_BC
SSD_GW = SSD_HPG * SSD_HEAD_DIM

MIX_WIDTH = RET_W + SSD_INNER
D_FF = 4 * D_MODEL

C_Q, C_K, C_V, C_G, C_Z, C_XBC, C_DT = 0, 1024, 2048, 3072, 4096, 5120, 7168

HIST = 8

MIXER_TILE = 256
FFN_TILE = 512
FFN_SLAB = 1024
VMEM_LIMIT = 56 * 1024 * 1024


def _silu(v):
    return v * jax.nn.sigmoid(v)


def _softplus(v):
    return jnp.maximum(v, 0.0) + jnp.log(1.0 + jnp.exp(-jnp.abs(v)))


def _dot(a, b):
    return jnp.dot(a, b, preferred_element_type=F32)


def _dot_nt(a, b):
    return lax.dot_general(a, b, (((1,), (1,)), ((), ())),
                           preferred_element_type=F32)


def _dot_tn(a, b):
    return lax.dot_general(a, b, (((0,), (0,)), ((), ())),
                           preferred_element_type=F32)


def _mixer_kernel(*refs, tt, masked, has_init, emit_state):
    it = iter(refs)
    x_ref = next(it)
    rope_ref = next(it)
    valid_ref = next(it) if masked else None
    n1_ref = next(it)
    wm_ref = next(it)
    wdt_ref = next(it)
    retw_ref = next(it)
    dq_ref = next(it)
    dk_ref = next(it)
    dmask_ref = next(it)
    cdec_ref = next(it)
    convw_ref = next(it)
    convb_ref = next(it)
    dtb_ref = next(it)
    alog_ref = next(it)
    dsk_ref = next(it)
    ssdw_ref = next(it)
    wout_ref = next(it)
    if has_init:
        s0_ref, h0_ref, c0_ref = next(it), next(it), next(it)
    out_ref = next(it)
    if emit_state:
        so_ref, ho_ref, co_ref = next(it), next(it), next(it)
    (qr_s, kr_s, qd_s, kd_s, v_s, gate_s, zs_s, xs_s, dt_s, b_s, c_s,
     xbuf, y_s, sret, hssd) = it

    t = pl.program_id(1)

    @pl.when(t == 0)
    def _init():
        if has_init:
            sret[...] = s0_ref[...]
            hssd[...] = h0_ref[...]
            xbuf[0:HIST, :] = c0_ref[...]
        else:
            sret[...] = jnp.zeros_like(sret)
            hssd[...] = jnp.zeros_like(hssd)
            xbuf[0:HIST, :] = jnp.zeros((HIST, SSD_CONV_DIM), F32)

    n_chunks = tt // CHUNK
    chunks = [slice(c * CHUNK, (c + 1) * CHUNK) for c in range(n_chunks)]
    heads = [slice(RET_DK * h, RET_DK * (h + 1)) for h in range(RET_HEADS)]
    glanes = [slice(SSD_GW * g, SSD_GW * (g + 1)) for g in range(SSD_GROUPS)]
    nlanes = [slice(SSD_STATE * g, SSD_STATE * (g + 1)) for g in range(SSD_GROUPS)]

    x = x_ref[0]
    ms = jnp.mean(x * x, axis=-1, keepdims=True)
    xn = ((x * lax.rsqrt(ms + EPS)) * n1_ref[...]).astype(BF16)

    row = lax.broadcasted_iota(jnp.int32, (CHUNK, SSD_INNER), 0)
    col = lax.broadcasted_iota(jnp.int32, (CHUNK, SSD_INNER), 1) & (CHUNK - 1)
    tri_r = lax.broadcasted_iota(jnp.int32, (tt, tt), 0)
    tri_c = lax.broadcasted_iota(jnp.int32, (tt, tt), 1)
    tril = jnp.logical_and(tri_c <= tri_r,
                           tri_c // CHUNK == tri_r // CHUNK).astype(BF16)
    blk_r = lax.broadcasted_iota(jnp.int32, (SSD_GW, SSD_GW), 0) // SSD_HEAD_DIM
    blk_c = lax.broadcasted_iota(jnp.int32, (SSD_GW, SSD_GW), 1) // SSD_HEAD_DIM
    a_neg = -jnp.exp(alog_ref[...])

    dt = _softplus(_dot(xn, wdt_ref[...]) + dtb_ref[...])
    if masked:
        vcol = valid_ref[:, 0:1]
        dt = dt * vcol
    dt_s[...] = dt

    def conv_silu(lo, hi):
        xbuf[HIST:HIST + tt, lo:hi] = _dot(xn, wm_ref[:, C_XBC + lo:C_XBC + hi])
        conv = (convb_ref[:, lo:hi]
                + convw_ref[SSD_CONV - 1:SSD_CONV, lo:hi] * xbuf[HIST:HIST + tt, lo:hi])
        for j in range(SSD_CONV - 1):
            off = HIST - (SSD_CONV - 1) + j
            conv = conv + convw_ref[j:j + 1, lo:hi] * xbuf[off:off + tt, lo:hi]
        xbuf[0:HIST, lo:hi] = xbuf[tt:tt + HIST, lo:hi]
        act = _silu(conv)
        if masked:
            act = act * vcol
        return act

    xs_s[...] = conv_silu(0, SSD_INNER)
    bc = conv_silu(SSD_INNER, SSD_CONV_DIM)
    b_s[...] = bc[:, :SSD_BC].astype(BF16)
    c_s[...] = bc[:, SSD_BC:].astype(BF16)

    qc = kc = rope_ref[0]
    qs = ks = rope_ref[1]
    dq_t = jnp.concatenate([dq_ref[...]] * n_chunks, axis=0)
    dk_t = jnp.concatenate([dk_ref[...]] * n_chunks, axis=0)
    qf = _dot(xn, wm_ref[:, C_Q:C_K])
    kf = _dot(xn, wm_ref[:, C_K:C_V])
    for sl in heads:
        qh = qf[:, sl]
        qr = qh * qc + pltpu.roll(qh, RET_DK // 2, 1) * qs
        qr_s[:, sl] = qr.astype(BF16)
        qd_s[:, sl] = (qr * dq_t[:, sl]).astype(BF16)
        kh = kf[:, sl]
        kr = kh * kc + pltpu.roll(kh, RET_DK // 2, 1) * ks
        kr_s[:, sl] = kr.astype(BF16)
        kd_s[:, sl] = (kr * dk_t[:, sl]).astype(BF16)

    da = dt_s[...] * a_neg
    hi = da.astype(BF16)
    mid = (da - hi.astype(F32)).astype(BF16)
    acol_all = _dot(tril, hi) + _dot(tril, mid)
    v_s[...] = _dot(xn, wm_ref[:, C_V:C_G]).astype(BF16)
    cb = [[None] * SSD_GROUPS for _ in chunks]
    for c, rows in enumerate(chunks):
        for g in range(SSD_GROUPS):
            brep = jnp.concatenate([b_s[rows, nlanes[g]]] * SSD_HPG, axis=0)
            cb[c][g] = _dot_nt(c_s[rows, nlanes[g]], brep)

    gate_s[...] = _silu(_dot(xn, wm_ref[:, C_G:C_Z])) * retw_ref[...]

    lhs_o = [[None] * RET_HEADS for _ in chunks]
    kv = [[None] * RET_HEADS for _ in chunks]
    for c, rows in enumerate(chunks):
        for h in range(RET_HEADS):
            sl = heads[h]
            sc = _dot_nt(qr_s[rows, sl], kr_s[rows, sl])
            p = (sc * dmask_ref[h]).astype(BF16)
            lhs_o[c][h] = jnp.concatenate([qd_s[rows, sl], p], axis=1)
            kv[c][h] = _dot_tn(kd_s[rows, sl], v_s[rows, sl])

    zs_s[...] = _silu(_dot(xn, wm_ref[:, C_Z:C_XBC]))

    ydiag = [[None] * SSD_GROUPS for _ in chunks]
    dstate = [[None] * SSD_GROUPS for _ in chunks]
    exp_a, exp_last = [], []
    for c, rows in enumerate(chunks):
        acol = acol_all[rows]
        arow = jnp.sum(jnp.where(row == col, acol, 0.0), axis=0, keepdims=True)
        lmat = jnp.exp(jnp.where(col <= row, acol - arow, -1e30))
        a_last = acol[CHUNK - 1:CHUNK, :]
        exp_a.append(jnp.exp(acol))
        exp_last.append(jnp.exp(a_last))
        xdt = xs_s[rows, :] * dt_s[rows, :]
        xw = (xdt * jnp.exp(a_last - acol)).astype(BF16)
        xdt_b = xdt.astype(BF16)
        for g in range(SSD_GROUPS):
            wgt = (cb[c][g] * lmat[:, glanes[g]]).astype(BF16)
            xrep = jnp.concatenate([xdt_b[:, glanes[g]]] * SSD_HPG, axis=0)
            bd = jnp.where(blk_r == blk_c, xrep, jnp.zeros_like(xrep))
            ydiag[c][g] = _dot(wgt, bd)
            dstate[c][g] = _dot_tn(b_s[rows, nlanes[g]], xw[:, glanes[g]])

    st_in = [[None] * RET_HEADS for _ in chunks]
    for h in range(RET_HEADS):
        st = sret[h]
        for c in range(n_chunks):
            st_in[c][h] = st.astype(BF16)
            st = cdec_ref[h:h + 1, :] * st + kv[c][h]
        sret[h] = st
    hg_in = [[None] * SSD_GROUPS for _ in chunks]
    for g in range(SSD_GROUPS):
        hg = hssd[g]
        for c in range(n_chunks):
            hg_in[c][g] = hg.astype(BF16)
            hg = exp_last[c][:, glanes[g]] * hg + dstate[c][g]
        hssd[g] = hg

    for c, rows in enumerate(chunks):
        for h in range(RET_HEADS):
            o = _dot(lhs_o[c][h],
                     jnp.concatenate([st_in[c][h], v_s[rows, heads[h]]], axis=0))
            o = o * lax.rsqrt(jnp.mean(o * o, axis=-1, keepdims=True) + EPS)
            y_s[rows, heads[h]] = (o * gate_s[rows, heads[h]]).astype(BF16)
    acc = x + _dot(y_s[:, :RET_W], wout_ref[:RET_W, :])
    for c, rows in enumerate(chunks):
        for g in range(SSD_GROUPS):
            gl = glanes[g]
            yg = (ydiag[c][g]
                  + _dot(c_s[rows, nlanes[g]], hg_in[c][g]) * exp_a[c][:, gl]
                  + xs_s[rows, gl] * dsk_ref[:, gl])
            yg = yg * zs_s[rows, gl]
            yg = yg * lax.rsqrt(jnp.mean(yg * yg, axis=-1, keepdims=True) + EPS)
            y_s[rows, RET_W + SSD_GW * g:RET_W + SSD_GW * (g + 1)] = (
                yg * ssdw_ref[:, gl]).astype(BF16)

    out_ref[0] = acc + _dot(y_s[:, RET_W:], wout_ref[RET_W:, :])

    if emit_state:
        so_ref[...] = sret[...]
        ho_ref[...] = hssd[...]
        co_ref[...] = xbuf[0:HIST, :]


def _const_spec(shape, single=True):
    nd = len(shape)
    kw = {"pipeline_mode": pl.Buffered(1)} if single else {}
    return pl.BlockSpec(shape, lambda *_: (0,) * nd, **kw)


def _mixer_call(x, rope, valid, consts, init, *, tt, emit_state):
    b, n, _ = x.shape
    masked = valid is not None
    has_init = init is not None
    nt = n // tt

    in_specs = [pl.BlockSpec((1, tt, D_MODEL), lambda i, j: (i, j, 0)),
                pl.BlockSpec((2, tt, RET_DK), lambda i, j: (0, j, 0))]
    args = [x, rope]
    if masked:
        in_specs.append(pl.BlockSpec((tt, 128), lambda i, j: (j, 0)))
        args.append(valid)
    for c in consts:
        in_specs.append(_const_spec(c.shape))
        args.append(c)
    if has_init:
        for c in init:
            in_specs.append(_const_spec(c.shape))
            args.append(c)

    out_shape = [jax.ShapeDtypeStruct((b, n, D_MODEL), F32)]
    out_specs = [pl.BlockSpec((1, tt, D_MODEL), lambda i, j: (i, j, 0))]
    if emit_state:
        st_shapes = [(RET_HEADS, RET_DK, RET_DK), (SSD_GROUPS, SSD_STATE, SSD_GW),
                     (HIST, SSD_CONV_DIM)]
        for s in st_shapes:
            out_shape.append(jax.ShapeDtypeStruct(s, F32))
            out_specs.append(_const_spec(s, single=False))

    scratch = [pltpu.VMEM((tt, RET_W), BF16)] * 5
    scratch += [pltpu.VMEM((tt, SSD_INNER), F32)] * 4
    scratch += [pltpu.VMEM((tt, SSD_BC), BF16)] * 2
    scratch += [pltpu.VMEM((tt + HIST, SSD_CONV_DIM), F32),
                pltpu.VMEM((tt, MIX_WIDTH), BF16),
                pltpu.VMEM((RET_HEADS, RET_DK, RET_DK), F32),
                pltpu.VMEM((SSD_GROUPS, SSD_STATE, SSD_GW), F32)]

    kern = functools.partial(_mixer_kernel, tt=tt, masked=masked,
                             has_init=has_init, emit_state=emit_state)
    return pl.pallas_call(
        kern,
        grid=(b, nt),
        in_specs=in_specs,
        out_specs=out_specs,
        out_shape=out_shape,
        scratch_shapes=scratch,
        compiler_params=pltpu.CompilerParams(
            dimension_semantics=("arbitrary", "arbitrary"),
            vmem_limit_bytes=VMEM_LIMIT),
        name="mixer_meta" if emit_state else "mixer",
    )(*args)


def _ffn_kernel(h_ref, n2_ref, w1_ref, w2_ref, fn_ref, out_ref):
    h = h_ref[...]
    ms = jnp.mean(h * h, axis=-1, keepdims=True)
    u = ((h * lax.rsqrt(ms + EPS)) * n2_ref[...]).astype(BF16)
    acc = h
    for j in range(D_FF // FFN_SLAB):
        sl = slice(FFN_SLAB * j, FFN_SLAB * (j + 1))
        a = jnp.maximum(_dot(u, w1_ref[:, sl]), 0.0)
        acc = acc + _dot((a * a).astype(BF16), w2_ref[sl, :])
    ms2 = jnp.mean(acc * acc, axis=-1, keepdims=True)
    out_ref[...] = (acc * lax.rsqrt(ms2 + EPS)) * fn_ref[...]


def _ffn_call(h, n2, w1, w2, fn, *, tm):
    n = h.shape[0]
    return pl.pallas_call(
        _ffn_kernel,
        grid=(n // tm,),
        in_specs=[pl.BlockSpec((tm, D_MODEL), lambda i: (i, 0)),
                  _const_spec(n2.shape), _const_spec(w1.shape),
                  _const_spec(w2.shape), _const_spec(fn.shape)],
        out_specs=pl.BlockSpec((tm, D_MODEL), lambda i: (i, 0)),
        out_shape=jax.ShapeDtypeStruct((n, D_MODEL), F32),
        compiler_params=pltpu.CompilerParams(
            dimension_semantics=("arbitrary",),
            vmem_limit_bytes=VMEM_LIMIT),
        name="ffn",
    )(h, n2, w1, w2, fn)


def _rope_tables(pos):
    half = RET_DK // 2
    freqs = ROPE_BASE ** (-np.arange(0, half, dtype=np.float64) / half)
    ang = np.asarray(pos, np.float64)[:, None] * freqs[None, :]
    cos, sin = np.cos(ang), np.sin(ang)
    return np.stack([np.concatenate([cos, cos], axis=-1),
                     np.concatenate([-sin, sin], axis=-1)]).astype(np.float32)


def _decay_tables():
    scale = RET_DK ** -0.5
    log_gamma = np.log1p(-np.exp2(-5.0 - np.arange(RET_HEADS, dtype=np.float64)))
    idx = np.arange(CHUNK, dtype=np.float64)
    dmask = scale * np.exp(log_gamma[:, None, None]
                           * np.abs(idx[:, None] - idx[None, :])[None])
    dq = scale * np.exp((idx + 1.0)[:, None] * log_gamma[None, :])
    dk = np.exp((CHUNK - 1.0 - idx)[:, None] * log_gamma[None, :])
    cdec = np.exp(CHUNK * log_gamma)[:, None]
    rep = lambda v: np.repeat(v, RET_DK, axis=-1).astype(np.float32)
    return rep(dq), rep(dk), dmask.astype(np.float32), rep(cdec)


def _lane_rep(v, n):
    return jnp.broadcast_to(v[..., None], v.shape + (n,)).reshape(
        v.shape[:-1] + (v.shape[-1] * n,))


def kernel(x, meta_tokens, norm1_w, w_in, ret_norm_w, conv_w, conv_b, dt_bias,
           a_log, d_skip, ssd_norm_w, w_out, norm2_w, w_ff1, w_ff2, final_norm_w):
    b, seq, d = x.shape
    assert d == D_MODEL and w_in.shape[0] == 1, "single-layer block only"
    tt = min(MIXER_TILE, seq)
    assert seq % tt == 0 and tt % CHUNK == 0

    w = w_in[0]
    w_main = w[:, :C_DT].astype(BF16)
    w_dt = _lane_rep(w[:, C_DT:], SSD_HEAD_DIM).astype(BF16)
    row2 = lambda v: v.reshape(1, -1).astype(F32)
    hrow = lambda v: _lane_rep(v.reshape(1, -1).astype(F32), SSD_HEAD_DIM)

    dq, dk, dmask, cdec = _decay_tables()

    consts = [row2(norm1_w[0]), w_main, w_dt, row2(ret_norm_w[0]), dq, dk, dmask,
              cdec, conv_w[0].astype(F32), row2(conv_b[0]), hrow(dt_bias[0]),
              hrow(a_log[0]), hrow(d_skip[0]), row2(ssd_norm_w[0]),
              w_out[0].astype(BF16)]

    m_idx = np.arange(CHUNK)
    x_meta = jnp.concatenate(
        [jnp.zeros((PAD, D_MODEL), x.dtype), meta_tokens.astype(x.dtype)])[None]
    valid = np.broadcast_to((m_idx >= PAD).astype(np.float32)[:, None], (CHUNK, 128))
    rope_meta = _rope_tables(m_idx - PAD)
    _, s0, h0, c0 = _mixer_call(x_meta, rope_meta, valid, consts, None,
                                tt=CHUNK, emit_state=True)

    rope = _rope_tables(np.arange(seq) + N_META)
    (h1,) = _mixer_call(x, rope, None, consts, (s0, h0, c0),
                        tt=tt, emit_state=False)

    tm = min(FFN_TILE, b * seq)
    out = _ffn_call(h1.reshape(b * seq, D_MODEL), row2(norm2_w[0]),
                    w_ff1[0].astype(BF16), w_ff2[0].astype(BF16),
                    row2(final_norm_w), tm=tm)
    return out.reshape(b, seq, D_MODEL)
```

```python
import functools

import jax
import jax.numpy as jnp
import numpy as np
from jax import lax
from jax.experimental import pallas as pl
from jax.experimental.pallas import tpu as pltpu

F32 = jnp.float32
BF16 = jnp.bfloat16

D_MODEL = 1024
CHUNK = 64
N_META = 16
PAD = CHUNK - N_META
EPS = 1e-6
ROPE_BASE = 10000.0

RET_HEADS = 8
RET_DK = 128
RET_W = RET_HEADS * RET_DK

SSD_INNER = 1024
SSD_HEAD_DIM = 64
SSD_HEADS = SSD_INNER // SSD_HEAD_DIM
SSD_GROUPS = 4
SSD_HPG = SSD_HEADS // SSD_GROUPS
SSD_STATE = 128
SSD_CONV = 4
SSD_BC = SSD_GROUPS * SSD_STATE
SSD_CONV_DIM = SSD_INNER + 2 * SSD_BC
SSD_GW = SSD_HPG * SSD_HEAD_DIM

MIX_WIDTH = RET_W + SSD_INNER
D_FF = 4 * D_MODEL

C_Q, C_K, C_V, C_G, C_Z, C_XBC, C_DT = 0, 1024, 2048, 3072, 4096, 5120, 7168

HIST = 8

MIXER_TILE = 256
FFN_TILE = 1024
FFN_SLAB = 1024
VMEM_LIMIT = 56 * 1024 * 1024


def _silu(v):
    h = 0.5 * v
    return h + h * jnp.tanh(h)


def _softplus(v):
    return jnp.maximum(v, 0.0) + jnp.log(1.0 + jnp.exp(-jnp.abs(v)))


def _dot(a, b):
    return jnp.dot(a, b, preferred_element_type=F32)


def _dot_nt(a, b):
    return lax.dot_general(a, b, (((1,), (1,)), ((), ())),
                           preferred_element_type=F32)


def _dot_tn(a, b):
    return lax.dot_general(a, b, (((0,), (0,)), ((), ())),
                           preferred_element_type=F32)


def _mixer_kernel(*refs, tt, masked, has_init, emit_state):
    it = iter(refs)
    x_ref = next(it)
    rope_ref = next(it)
    valid_ref = next(it) if masked else None
    n1_ref = next(it)
    wm_ref = next(it)
    wdt_ref = next(it)
    retw_ref = next(it)
    dq_ref = next(it)
    dk_ref = next(it)
    dmask_ref = next(it)
    cdec_ref = next(it)
    convw_ref = next(it)
    convb_ref = next(it)
    dtb_ref = next(it)
    alog_ref = next(it)
    dsk_ref = next(it)
    ssdw_ref = next(it)
    wout_ref = next(it)
    if has_init:
        s0_ref, h0_ref, c0_ref = next(it), next(it), next(it)
    out_ref = next(it)
    if emit_state:
        so_ref, ho_ref, co_ref = next(it), next(it), next(it)
    (qr_s, kr_s, qd_s, kd_s, v_s, gate_s, zs_s, xs_s, dt_s, b_s, c_s,
     xbuf, y_s, sret, hssd) = it

    t = pl.program_id(1)

    @pl.when(t == 0)
    def _init():
        if has_init:
            sret[...] = s0_ref[...]
            hssd[...] = h0_ref[...]
            xbuf[0:HIST, :] = c0_ref[...]
        else:
            sret[...] = jnp.zeros_like(sret)
            hssd[...] = jnp.zeros_like(hssd)
            xbuf[0:HIST, :] = jnp.zeros((HIST, SSD_CONV_DIM), F32)

    n_chunks = tt // CHUNK
    chunks = [slice(c * CHUNK, (c + 1) * CHUNK) for c in range(n_chunks)]
    heads = [slice(RET_DK * h, RET_DK * (h + 1)) for h in range(RET_HEADS)]
    glanes = [slice(SSD_GW * g, SSD_GW * (g + 1)) for g in range(SSD_GROUPS)]
    nlanes = [slice(SSD_STATE * g, SSD_STATE * (g + 1)) for g in range(SSD_GROUPS)]

    x = x_ref[0]
    ms = jnp.mean(x * x, axis=-1, keepdims=True)
    xn = ((x * lax.rsqrt(ms + EPS)) * n1_ref[...]).astype(BF16)

    row = lax.broadcasted_iota(jnp.int32, (CHUNK, SSD_INNER), 0)
    col = lax.broadcasted_iota(jnp.int32, (CHUNK, SSD_INNER), 1) & (CHUNK - 1)
    tri_r = lax.broadcasted_iota(jnp.int32, (tt, tt), 0)
    tri_c = lax.broadcasted_iota(jnp.int32, (tt, tt), 1)
    tril = jnp.logical_and(tri_c <= tri_r,
                           tri_c // CHUNK == tri_r // CHUNK).astype(BF16)
    blk_r = lax.broadcasted_iota(jnp.int32, (SSD_GW, SSD_GW), 0) // SSD_HEAD_DIM
    blk_c = lax.broadcasted_iota(jnp.int32, (SSD_GW, SSD_GW), 1) // SSD_HEAD_DIM
    a_neg = -jnp.exp(alog_ref[...])

    dt = _softplus(_dot(xn, wdt_ref[...]) + dtb_ref[...])
    if masked:
        vcol = valid_ref[:, 0:1]
        dt = dt * vcol
    dt_s[...] = dt

    def conv_silu(lo, hi):
        xbuf[HIST:HIST + tt, lo:hi] = _dot(xn, wm_ref[:, C_XBC + lo:C_XBC + hi])
        conv = (convb_ref[:, lo:hi]
                + convw_ref[SSD_CONV - 1:SSD_CONV, lo:hi] * xbuf[HIST:HIST + tt, lo:hi])
        for j in range(SSD_CONV - 1):
            off = HIST - (SSD_CONV - 1) + j
            conv = conv + convw_ref[j:j + 1, lo:hi] * xbuf[off:off + tt, lo:hi]
        xbuf[0:HIST, lo:hi] = xbuf[tt:tt + HIST, lo:hi]
        act = _silu(conv)
        if masked:
            act = act * vcol
        return act

    xs_s[...] = conv_silu(0, SSD_INNER)
    bc = conv_silu(SSD_INNER, SSD_CONV_DIM)
    b_s[...] = bc[:, :SSD_BC].astype(BF16)
    c_s[...] = bc[:, SSD_BC:].astype(BF16)

    cos2, sin2 = rope_ref[0], rope_ref[1]
    dq_t = jnp.concatenate([dq_ref[...]] * n_chunks, axis=0)
    dk_t = jnp.concatenate([dk_ref[...]] * n_chunks, axis=0)
    qf = _dot(xn, wm_ref[:, C_Q:C_K])
    kf = _dot(xn, wm_ref[:, C_K:C_V])
    for sl in heads:
        qh = qf[:, sl]
        qr = qh * cos2 + pltpu.roll(qh, RET_DK // 2, 1) * sin2
        qr_s[:, sl] = qr.astype(BF16)
        qd_s[:, sl] = (qr * dq_t[:, sl]).astype(BF16)
        kh = kf[:, sl]
        kr = kh * cos2 + pltpu.roll(kh, RET_DK // 2, 1) * sin2
        kr_s[:, sl] = kr.astype(BF16)
        kd_s[:, sl] = (kr * dk_t[:, sl]).astype(BF16)

    da = dt_s[...] * a_neg
    hi = da.astype(BF16)
    mid = (da - hi.astype(F32)).astype(BF16)
    acol_all = _dot(tril, hi) + _dot(tril, mid)
    v_s[...] = _dot(xn, wm_ref[:, C_V:C_G]).astype(BF16)
    cb = [[None] * SSD_GROUPS for _ in chunks]
    for c, rows in enumerate(chunks):
        for g in range(SSD_GROUPS):
            brep = jnp.concatenate([b_s[rows, nlanes[g]]] * SSD_HPG, axis=0)
            cb[c][g] = _dot_nt(c_s[rows, nlanes[g]], brep)

    gate_s[...] = _silu(_dot(xn, wm_ref[:, C_G:C_Z])) * retw_ref[...]

    lhs_o = [[None] * RET_HEADS for _ in chunks]
    kv = [[None] * RET_HEADS for _ in chunks]
    for c, rows in enumerate(chunks):
        for h in range(RET_HEADS):
            sl = heads[h]
            sc = _dot_nt(qr_s[rows, sl], kr_s[rows, sl])
            p = (sc * dmask_ref[h]).astype(BF16)
            lhs_o[c][h] = jnp.concatenate([qd_s[rows, sl], p], axis=1)
            kv[c][h] = _dot_tn(kd_s[rows, sl], v_s[rows, sl])

    zs_s[...] = _silu(_dot(xn, wm_ref[:, C_Z:C_XBC]))

    ydiag = [[None] * SSD_GROUPS for _ in chunks]
    dstate = [[None] * SSD_GROUPS for _ in chunks]
    exp_a, exp_last = [], []
    for c, rows in enumerate(chunks):
        acol = acol_all[rows]
        arow = jnp.sum(jnp.where(row == col, acol, 0.0), axis=0, keepdims=True)
        lmat = jnp.exp(jnp.where(col <= row, acol - arow, -1e30))
        a_last = acol[CHUNK - 1:CHUNK, :]
        exp_a.append(jnp.exp(acol))
        exp_last.append(jnp.exp(a_last))
        xdt = xs_s[rows, :] * dt_s[rows, :]
        xw = (xdt * jnp.exp(a_last - acol)).astype(BF16)
        xdt_b = xdt.astype(BF16)
        for g in range(SSD_GROUPS):
            wgt = (cb[c][g] * lmat[:, glanes[g]]).astype(BF16)
            xrep = jnp.concatenate([xdt_b[:, glanes[g]]] * SSD_HPG, axis=0)
            bd = jnp.where(blk_r == blk_c, xrep, jnp.zeros_like(xrep))
            ydiag[c][g] = _dot(wgt, bd)
            dstate[c][g] = _dot_tn(b_s[rows, nlanes[g]], xw[:, glanes[g]])

    st_in = [[None] * RET_HEADS for _ in chunks]
    for h in range(RET_HEADS):
        st = sret[h]
        for c in range(n_chunks):
            st_in[c][h] = st.astype(BF16)
            st = cdec_ref[h:h + 1, :] * st + kv[c][h]
        sret[h] = st
    hg_in = [[None] * SSD_GROUPS for _ in chunks]
    for g in range(SSD_GROUPS):
        hg = hssd[g]
        for c in range(n_chunks):
            hg_in[c][g] = hg.astype(BF16)
            hg = exp_last[c][:, glanes[g]] * hg + dstate[c][g]
        hssd[g] = hg

    for c, rows in enumerate(chunks):
        for h in range(RET_HEADS):
            o = _dot(lhs_o[c][h],
                     jnp.concatenate([st_in[c][h], v_s[rows, heads[h]]], axis=0))
            o = o * lax.rsqrt(jnp.mean(o * o, axis=-1, keepdims=True) + EPS)
            y_s[rows, heads[h]] = (o * gate_s[rows, heads[h]]).astype(BF16)
    acc = x + _dot(y_s[:, :RET_W], wout_ref[:RET_W, :])
    for c, rows in enumerate(chunks):
        for g in range(SSD_GROUPS):
            gl = glanes[g]
            yg = (ydiag[c][g]
                  + _dot(c_s[rows, nlanes[g]], hg_in[c][g]) * exp_a[c][:, gl]
                  + xs_s[rows, gl] * dsk_ref[:, gl])
            yg = yg * zs_s[rows, gl]
            yg = yg * lax.rsqrt(jnp.mean(yg * yg, axis=-1, keepdims=True) + EPS)
            y_s[rows, RET_W + SSD_GW * g:RET_W + SSD_GW * (g + 1)] = (
                yg * ssdw_ref[:, gl]).astype(BF16)

    out_ref[0] = acc + _dot(y_s[:, RET_W:], wout_ref[RET_W:, :])

    if emit_state:
        so_ref[...] = sret[...]
        ho_ref[...] = hssd[...]
        co_ref[...] = xbuf[0:HIST, :]


def _const_spec(shape, single=True):
    nd = len(shape)
    kw = {"pipeline_mode": pl.Buffered(1)} if single else {}
    return pl.BlockSpec(shape, lambda *_: (0,) * nd, **kw)


def _mixer_call(x, rope, valid, consts, init, *, tt, emit_state):
    b, n, _ = x.shape
    masked = valid is not None
    has_init = init is not None
    nt = n // tt

    in_specs = [pl.BlockSpec((1, tt, D_MODEL), lambda i, j: (i, j, 0)),
                pl.BlockSpec((2, tt, RET_DK), lambda i, j: (0, j, 0))]
    args = [x, rope]
    if masked:
        in_specs.append(pl.BlockSpec((tt, 128), lambda i, j: (j, 0)))
        args.append(valid)
    for c in consts:
        in_specs.append(_const_spec(c.shape))
        args.append(c)
    if has_init:
        for c in init:
            in_specs.append(_const_spec(c.shape))
            args.append(c)

    out_shape = [jax.ShapeDtypeStruct((b, n, D_MODEL), F32)]
    out_specs = [pl.BlockSpec((1, tt, D_MODEL), lambda i, j: (i, j, 0))]
    if emit_state:
        st_shapes = [(RET_HEADS, RET_DK, RET_DK), (SSD_GROUPS, SSD_STATE, SSD_GW),
                     (HIST, SSD_CONV_DIM)]
        for s in st_shapes:
            out_shape.append(jax.ShapeDtypeStruct(s, F32))
            out_specs.append(_const_spec(s, single=False))

    scratch = [pltpu.VMEM((tt, RET_W), BF16)] * 5
    scratch += [pltpu.VMEM((tt, SSD_INNER), F32)] * 4
    scratch += [pltpu.VMEM((tt, SSD_BC), BF16)] * 2
    scratch += [pltpu.VMEM((tt + HIST, SSD_CONV_DIM), F32),
                pltpu.VMEM((tt, MIX_WIDTH), BF16),
                pltpu.VMEM((RET_HEADS, RET_DK, RET_DK), F32),
                pltpu.VMEM((SSD_GROUPS, SSD_STATE, SSD_GW), F32)]

    kern = functools.partial(_mixer_kernel, tt=tt, masked=masked,
                             has_init=has_init, emit_state=emit_state)
    return pl.pallas_call(
        kern,
        grid=(b, nt),
        in_specs=in_specs,
        out_specs=out_specs,
        out_shape=out_shape,
        scratch_shapes=scratch,
        compiler_params=pltpu.CompilerParams(
            dimension_semantics=("arbitrary", "arbitrary"),
            vmem_limit_bytes=VMEM_LIMIT),
        name="mixer_meta" if emit_state else "mixer",
    )(*args)


def _ffn_kernel(h_ref, n2_ref, w1_ref, w2_ref, fn_ref, out_ref):
    h = h_ref[...]
    ms = jnp.mean(h * h, axis=-1, keepdims=True)
    u = ((h * lax.rsqrt(ms + EPS)) * n2_ref[...]).astype(BF16)
    acc = h
    for j in range(D_FF // FFN_SLAB):
        sl = slice(FFN_SLAB * j, FFN_SLAB * (j + 1))
        a = jnp.maximum(_dot(u, w1_ref[:, sl]), 0.0)
        acc = acc + _dot((a * a).astype(BF16), w2_ref[sl, :])
    ms2 = jnp.mean(acc * acc, axis=-1, keepdims=True)
    out_ref[...] = (acc * lax.rsqrt(ms2 + EPS)) * fn_ref[...]


def _ffn_call(h, n2, w1, w2, fn, *, tm):
    n = h.shape[0]
    return pl.pallas_call(
        _ffn_kernel,
        grid=(n // tm,),
        in_specs=[pl.BlockSpec((tm, D_MODEL), lambda i: (i, 0)),
                  _const_spec(n2.shape), _const_spec(w1.shape),
                  _const_spec(w2.shape), _const_spec(fn.shape)],
        out_specs=pl.BlockSpec((tm, D_MODEL), lambda i: (i, 0)),
        out_shape=jax.ShapeDtypeStruct((n, D_MODEL), F32),
        compiler_params=pltpu.CompilerParams(
            dimension_semantics=("arbitrary",),
            vmem_limit_bytes=VMEM_LIMIT),
        name="ffn",
    )(h, n2, w1, w2, fn)


def _rope_tables(pos):
    half = RET_DK // 2
    freqs = ROPE_BASE ** (-np.arange(0, half, dtype=np.float64) / half)
    ang = np.asarray(pos, np.float64)[:, None] * freqs[None, :]
    cos, sin = np.cos(ang), np.sin(ang)
    return np.stack([np.concatenate([cos, cos], axis=-1),
                     np.concatenate([-sin, sin], axis=-1)]).astype(np.float32)


def _decay_tables():
    scale = RET_DK ** -0.5
    log_gamma = np.log1p(-np.exp2(-5.0 - np.arange(RET_HEADS, dtype=np.float64)))
    idx = np.arange(CHUNK, dtype=np.float64)
    dmask = scale * np.exp(log_gamma[:, None, None]
                           * np.abs(idx[:, None] - idx[None, :])[None])
    dq = scale * np.exp((idx + 1.0)[:, None] * log_gamma[None, :])
    dk = np.exp((CHUNK - 1.0 - idx)[:, None] * log_gamma[None, :])
    cdec = np.exp(CHUNK * log_gamma)[:, None]
    rep = lambda v: np.repeat(v, RET_DK, axis=-1).astype(np.float32)
    return rep(dq), rep(dk), dmask.astype(np.float32), rep(cdec)


def _lane_rep(v, n):
    return jnp.broadcast_to(v[..., None], v.shape + (n,)).reshape(
        v.shape[:-1] + (v.shape[-1] * n,))


def kernel(x, meta_tokens, norm1_w, w_in, ret_norm_w, conv_w, conv_b, dt_bias,
           a_log, d_skip, ssd_norm_w, w_out, norm2_w, w_ff1, w_ff2, final_norm_w):
    b, seq, d = x.shape
    assert d == D_MODEL and w_in.shape[0] == 1, "single-layer block only"
    tt = min(MIXER_TILE, seq)
    assert seq % tt == 0 and tt % CHUNK == 0

    w = w_in[0]
    w_main = w.astype(BF16)
    w_dt = _lane_rep(w[:, C_DT:], SSD_HEAD_DIM).astype(BF16)
    row2 = lambda v: v.reshape(1, -1).astype(F32)
    hrow = lambda v: _lane_rep(v.reshape(1, -1).astype(F32), SSD_HEAD_DIM)

    dq, dk, dmask, cdec = _decay_tables()

    consts = [row2(norm1_w[0]), w_main, w_dt, row2(ret_norm_w[0]), dq, dk, dmask,
              cdec, conv_w[0].astype(F32), row2(conv_b[0]), hrow(dt_bias[0]),
              hrow(a_log[0]), hrow(d_skip[0]), row2(ssd_norm_w[0]),
              w_out[0].astype(BF16)]

    m_idx = np.arange(CHUNK)
    x_meta = jnp.concatenate(
        [jnp.zeros((PAD, D_MODEL), x.dtype), meta_tokens.astype(x.dtype)])[None]
    valid = np.broadcast_to((m_idx >= PAD).astype(np.float32)[:, None], (CHUNK, 128))
    rope_meta = _rope_tables(m_idx - PAD)
    _, s0, h0, c0 = _mixer_call(x_meta, rope_meta, valid, consts, None,
                                tt=CHUNK, emit_state=True)

    rope = _rope_tables(np.arange(seq) + N_META)
    (h1,) = _mixer_call(x, rope, None, consts, (s0, h0, c0),
                        tt=tt, emit_state=False)

    tm = min(FFN_TILE, b * seq)
    out = _ffn_call(h1.reshape(b * seq, D_MODEL), row2(norm2_w[0]),
                    w_ff1[0].astype(BF16), w_ff2[0].astype(BF16),
                    row2(final_norm_w), tm=tm)
    return out.reshape(b, seq, D_MODEL)
```

```python
import functools

import jax
import jax.numpy as jnp
import numpy as np
from jax import lax
from jax.experimental import pallas as pl
from jax.experimental.pallas import tpu as pltpu

F32 = jnp.float32
BF16 = jnp.bfloat16

D_MODEL = 1024
CHUNK = 64
N_META = 16
PAD = CHUNK - N_META
EPS = 1e-6
ROPE_BASE = 10000.0

RET_HEADS = 8
RET_DK = 128
RET_W = RET_HEADS * RET_DK

SSD_INNER = 1024
SSD_HEAD_DIM = 64
SSD_HEADS = SSD_INNER // SSD_HEAD_DIM
SSD_GROUPS = 4
SSD_HPG = SSD_HEADS // SSD_GROUPS
SSD_STATE = 128
SSD_CONV = 4
SSD_BC = SSD_GROUPS * SSD_STATE
SSD_CONV_DIM = SSD_INNER + 2 * SSD_BC
SSD_GW = SSD_HPG * SSD_HEAD_DIM

MIX_WIDTH = RET_W + SSD_INNER
D_FF = 4 * D_MODEL

C_Q, C_K, C_V, C_G, C_Z, C_XBC, C_DT = 0, 1024, 2048, 3072, 4096, 5120, 7168

HIST = 8

MIXER_TILE = 256
FFN_TILE = 1024
FFN_SLAB = 1024
VMEM_LIMIT = 56 * 1024 * 1024


def _silu(v):
    h = 0.5 * v
    return h + h * jnp.tanh(h)


def _softplus(v):
    return jnp.maximum(v, 0.0) + jnp.log(1.0 + jnp.exp(-jnp.abs(v)))


def _dot(a, b):
    return jnp.dot(a, b, preferred_element_type=F32)


def _dot_nt(a, b):
    return lax.dot_general(a, b, (((1,), (1,)), ((), ())),
                           preferred_element_type=F32)


def _dot_tn(a, b):
    return lax.dot_general(a, b, (((0,), (0,)), ((), ())),
                           preferred_element_type=F32)


def _mixer_kernel(*refs, tt, masked, has_init, emit_state):
    it = iter(refs)
    x_ref = next(it)
    rope_ref = next(it)
    valid_ref = next(it) if masked else None
    n1_ref = next(it)
    wm_ref = next(it)
    wdt_ref = next(it)
    retw_ref = next(it)
    dq_ref = next(it)
    dk_ref = next(it)
    dmask_ref = next(it)
    cdec_ref = next(it)
    convw_ref = next(it)
    convb_ref = next(it)
    dtb_ref = next(it)
    alog_ref = next(it)
    dsk_ref = next(it)
    ssdw_ref = next(it)
    wout_ref = next(it)
    if has_init:
        s0_ref, h0_ref, c0_ref = next(it), next(it), next(it)
    out_ref = next(it)
    if emit_state:
        so_ref, ho_ref, co_ref = next(it), next(it), next(it)
    (qr_s, kr_s, qd_s, kd_s, v_s, gate_s, zs_s, xs_s, dt_s, b_s, c_s,
     xbuf, y_s, sret, hssd) = it

    t = pl.program_id(1)

    @pl.when(t == 0)
    def _init():
        if has_init:
            sret[...] = s0_ref[...]
            hssd[...] = h0_ref[...]
            xbuf[0:HIST, :] = c0_ref[...]
        else:
            sret[...] = jnp.zeros_like(sret)
            hssd[...] = jnp.zeros_like(hssd)
            xbuf[0:HIST, :] = jnp.zeros((HIST, SSD_CONV_DIM), F32)

    n_chunks = tt // CHUNK
    chunks = [slice(c * CHUNK, (c + 1) * CHUNK) for c in range(n_chunks)]
    heads = [slice(RET_DK * h, RET_DK * (h + 1)) for h in range(RET_HEADS)]
    glanes = [slice(SSD_GW * g, SSD_GW * (g + 1)) for g in range(SSD_GROUPS)]
    nlanes = [slice(SSD_STATE * g, SSD_STATE * (g + 1)) for g in range(SSD_GROUPS)]

    x = x_ref[0]
    ms = jnp.mean(x * x, axis=-1, keepdims=True)
    xn = ((x * lax.rsqrt(ms + EPS)) * n1_ref[...]).astype(BF16)

    row = lax.broadcasted_iota(jnp.int32, (CHUNK, SSD_INNER), 0)
    col = lax.broadcasted_iota(jnp.int32, (CHUNK, SSD_INNER), 1) & (CHUNK - 1)
    tri_r = lax.broadcasted_iota(jnp.int32, (tt, tt), 0)
    tri_c = lax.broadcasted_iota(jnp.int32, (tt, tt), 1)
    tril = jnp.logical_and(tri_c <= tri_r,
                           tri_c // CHUNK == tri_r // CHUNK).astype(BF16)
    blk_r = lax.broadcasted_iota(jnp.int32, (SSD_GW, SSD_GW), 0) // SSD_HEAD_DIM
    blk_c = lax.broadcasted_iota(jnp.int32, (SSD_GW, SSD_GW), 1) // SSD_HEAD_DIM
    a_neg = -jnp.exp(alog_ref[...])
    lane = lax.broadcasted_iota(jnp.int32, (tt, 128), 1)

    def head_lanes(v):
        cols = []
        for j in range(SSD_HEADS // 2):
            lo = jnp.broadcast_to(v[:, 2 * j:2 * j + 1], (tt, 128))
            hi = jnp.broadcast_to(v[:, 2 * j + 1:2 * j + 2], (tt, 128))
            cols.append(jnp.where(lane < SSD_HEAD_DIM, lo, hi))
        return jnp.concatenate(cols, axis=1)

    dt_c = _softplus(_dot(xn, wdt_ref[...]) + dtb_ref[...])
    if masked:
        vcol = valid_ref[:, 0:1]
        dt_c = dt_c * vcol
    dt_s[...] = head_lanes(dt_c)
    da = dt_c * a_neg
    da_hi = da.astype(BF16)
    da_mid = (da - da_hi.astype(F32)).astype(BF16)

    def conv_silu(lo, hi):
        xbuf[HIST:HIST + tt, lo:hi] = _dot(xn, wm_ref[:, C_XBC + lo:C_XBC + hi])
        conv = (convb_ref[:, lo:hi]
                + convw_ref[SSD_CONV - 1:SSD_CONV, lo:hi] * xbuf[HIST:HIST + tt, lo:hi])
        for j in range(SSD_CONV - 1):
            off = HIST - (SSD_CONV - 1) + j
            conv = conv + convw_ref[j:j + 1, lo:hi] * xbuf[off:off + tt, lo:hi]
        xbuf[0:HIST, lo:hi] = xbuf[tt:tt + HIST, lo:hi]
        act = _silu(conv)
        if masked:
            act = act * vcol
        return act

    xs_s[...] = conv_silu(0, SSD_INNER)
    bc = conv_silu(SSD_INNER, SSD_CONV_DIM)
    b_s[...] = bc[:, :SSD_BC].astype(BF16)
    c_s[...] = bc[:, SSD_BC:].astype(BF16)

    cos2, sin2 = rope_ref[0], rope_ref[1]
    dq_t = jnp.concatenate([dq_ref[...]] * n_chunks, axis=0)
    dk_t = jnp.concatenate([dk_ref[...]] * n_chunks, axis=0)
    qf = _dot(xn, wm_ref[:, C_Q:C_K])
    kf = _dot(xn, wm_ref[:, C_K:C_V])
    for sl in heads:
        qh = qf[:, sl]
        qr = qh * cos2 + pltpu.roll(qh, RET_DK // 2, 1) * sin2
        qr_s[:, sl] = qr.astype(BF16)
        qd_s[:, sl] = (qr * dq_t[:, sl]).astype(BF16)
        kh = kf[:, sl]
        kr = kh * cos2 + pltpu.roll(kh, RET_DK // 2, 1) * sin2
        kr_s[:, sl] = kr.astype(BF16)
        kd_s[:, sl] = (kr * dk_t[:, sl]).astype(BF16)

    acol_all = head_lanes(_dot(tril, da_hi) + _dot(tril, da_mid))
    v_s[...] = _dot(xn, wm_ref[:, C_V:C_G]).astype(BF16)
    cb = [[None] * SSD_GROUPS for _ in chunks]
    for c, rows in enumerate(chunks):
        for g in range(SSD_GROUPS):
            brep = jnp.concatenate([b_s[rows, nlanes[g]]] * SSD_HPG, axis=0)
            cb[c][g] = _dot_nt(c_s[rows, nlanes[g]], brep)

    gate_s[...] = _silu(_dot(xn, wm_ref[:, C_G:C_Z])) * retw_ref[...]

    lhs_o = [[None] * RET_HEADS for _ in chunks]
    kv = [[None] * RET_HEADS for _ in chunks]
    for c, rows in enumerate(chunks):
        for h in range(RET_HEADS):
            sl = heads[h]
            sc = _dot_nt(qr_s[rows, sl], kr_s[rows, sl])
            p = (sc * dmask_ref[h]).astype(BF16)
            lhs_o[c][h] = jnp.concatenate([qd_s[rows, sl], p], axis=1)
            kv[c][h] = _dot_tn(kd_s[rows, sl], v_s[rows, sl])

    zs_s[...] = _silu(_dot(xn, wm_ref[:, C_Z:C_XBC]))

    ydiag = [[None] * SSD_GROUPS for _ in chunks]
    dstate = [[None] * SSD_GROUPS for _ in chunks]
    exp_a, exp_last = [], []
    for c, rows in enumerate(chunks):
        acol = acol_all[rows]
        arow = jnp.sum(jnp.where(row == col, acol, 0.0), axis=0, keepdims=True)
        lmat = jnp.exp(jnp.where(col <= row, acol - arow, -1e30))
        a_last = acol[CHUNK - 1:CHUNK, :]
        exp_a.append(jnp.exp(acol))
        exp_last.append(jnp.exp(a_last))
        xdt = xs_s[rows, :] * dt_s[rows, :]
        xw = (xdt * jnp.exp(a_last - acol)).astype(BF16)
        xdt_b = xdt.astype(BF16)
        for g in range(SSD_GROUPS):
            wgt = (cb[c][g] * lmat[:, glanes[g]]).astype(BF16)
            xrep = jnp.concatenate([xdt_b[:, glanes[g]]] * SSD_HPG, axis=0)
            bd = jnp.where(blk_r == blk_c, xrep, jnp.zeros_like(xrep))
            ydiag[c][g] = _dot(wgt, bd)
            dstate[c][g] = _dot_tn(b_s[rows, nlanes[g]], xw[:, glanes[g]])

    st_in = [[None] * RET_HEADS for _ in chunks]
    for h in range(RET_HEADS):
        st = sret[h]
        for c in range(n_chunks):
            st_in[c][h] = st.astype(BF16)
            st = cdec_ref[h:h + 1, :] * st + kv[c][h]
        sret[h] = st
    hg_in = [[None] * SSD_GROUPS for _ in chunks]
    for g in range(SSD_GROUPS):
        hg = hssd[g]
        for c in range(n_chunks):
            hg_in[c][g] = hg.astype(BF16)
            hg = exp_last[c][:, glanes[g]] * hg + dstate[c][g]
        hssd[g] = hg

    for c, rows in enumerate(chunks):
        for h in range(RET_HEADS):
            o = _dot(lhs_o[c][h],
                     jnp.concatenate([st_in[c][h], v_s[rows, heads[h]]], axis=0))
            o = o * lax.rsqrt(jnp.mean(o * o, axis=-1, keepdims=True) + EPS)
            y_s[rows, heads[h]] = (o * gate_s[rows, heads[h]]).astype(BF16)
    acc = x + _dot(y_s[:, :RET_W], wout_ref[:RET_W, :])
    for c, rows in enumerate(chunks):
        for g in range(SSD_GROUPS):
            gl = glanes[g]
            yg = (ydiag[c][g]
                  + _dot(c_s[rows, nlanes[g]], hg_in[c][g]) * exp_a[c][:, gl]
                  + xs_s[rows, gl] * dsk_ref[:, gl])
            yg = yg * zs_s[rows, gl]
            yg = yg * lax.rsqrt(jnp.mean(yg * yg, axis=-1, keepdims=True) + EPS)
            y_s[rows, RET_W + SSD_GW * g:RET_W + SSD_GW * (g + 1)] = (
                yg * ssdw_ref[:, gl]).astype(BF16)

    out_ref[0] = acc + _dot(y_s[:, RET_W:], wout_ref[RET_W:, :])

    if emit_state:
        so_ref[...] = sret[...]
        ho_ref[...] = hssd[...]
        co_ref[...] = xbuf[0:HIST, :]


def _const_spec(shape, single=True):
    nd = len(shape)
    kw = {"pipeline_mode": pl.Buffered(1)} if single else {}
    return pl.BlockSpec(shape, lambda *_: (0,) * nd, **kw)


def _mixer_call(x, rope, valid, consts, init, *, tt, emit_state):
    b, n, _ = x.shape
    masked = valid is not None
    has_init = init is not None
    nt = n // tt

    in_specs = [pl.BlockSpec((1, tt, D_MODEL), lambda i, j: (i, j, 0)),
                pl.BlockSpec((2, tt, RET_DK), lambda i, j: (0, j, 0))]
    args = [x, rope]
    if masked:
        in_specs.append(pl.BlockSpec((tt, 128), lambda i, j: (j, 0)))
        args.append(valid)
    for c in consts:
        in_specs.append(_const_spec(c.shape))
        args.append(c)
    if has_init:
        for c in init:
            in_specs.append(_const_spec(c.shape))
            args.append(c)

    out_shape = [jax.ShapeDtypeStruct((b, n, D_MODEL), F32)]
    out_specs = [pl.BlockSpec((1, tt, D_MODEL), lambda i, j: (i, j, 0))]
    if emit_state:
        st_shapes = [(RET_HEADS, RET_DK, RET_DK), (SSD_GROUPS, SSD_STATE, SSD_GW),
                     (HIST, SSD_CONV_DIM)]
        for s in st_shapes:
            out_shape.append(jax.ShapeDtypeStruct(s, F32))
            out_specs.append(_const_spec(s, single=False))

    scratch = [pltpu.VMEM((tt, RET_W), BF16)] * 5
    scratch += [pltpu.VMEM((tt, SSD_INNER), F32)] * 4
    scratch += [pltpu.VMEM((tt, SSD_BC), BF16)] * 2
    scratch += [pltpu.VMEM((tt + HIST, SSD_CONV_DIM), F32),
                pltpu.VMEM((tt, MIX_WIDTH), BF16),
                pltpu.VMEM((RET_HEADS, RET_DK, RET_DK), F32),
                pltpu.VMEM((SSD_GROUPS, SSD_STATE, SSD_GW), F32)]

    kern = functools.partial(_mixer_kernel, tt=tt, masked=masked,
                             has_init=has_init, emit_state=emit_state)
    return pl.pallas_call(
        kern,
        grid=(b, nt),
        in_specs=in_specs,
        out_specs=out_specs,
        out_shape=out_shape,
        scratch_shapes=scratch,
        compiler_params=pltpu.CompilerParams(
            dimension_semantics=("arbitrary", "arbitrary"),
            vmem_limit_bytes=VMEM_LIMIT),
        name="mixer_meta" if emit_state else "mixer",
    )(*args)


def _ffn_kernel(h_ref, n2_ref, w1_ref, w2_ref, fn_ref, out_ref):
    h = h_ref[...]
    ms = jnp.mean(h * h, axis=-1, keepdims=True)
    u = ((h * lax.rsqrt(ms + EPS)) * n2_ref[...]).astype(BF16)
    acc = h
    for j in range(D_FF // FFN_SLAB):
        sl = slice(FFN_SLAB * j, FFN_SLAB * (j + 1))
        a = jnp.maximum(_dot(u, w1_ref[:, sl]), 0.0)
        acc = acc + _dot((a * a).astype(BF16), w2_ref[sl, :])
    ms2 = jnp.mean(acc * acc, axis=-1, keepdims=True)
    out_ref[...] = (acc * lax.rsqrt(ms2 + EPS)) * fn_ref[...]


def _ffn_call(h, n2, w1, w2, fn, *, tm):
    n = h.shape[0]
    return pl.pallas_call(
        _ffn_kernel,
        grid=(n // tm,),
        in_specs=[pl.BlockSpec((tm, D_MODEL), lambda i: (i, 0)),
                  _const_spec(n2.shape), _const_spec(w1.shape),
                  _const_spec(w2.shape), _const_spec(fn.shape)],
        out_specs=pl.BlockSpec((tm, D_MODEL), lambda i: (i, 0)),
        out_shape=jax.ShapeDtypeStruct((n, D_MODEL), F32),
        compiler_params=pltpu.CompilerParams(
            dimension_semantics=("arbitrary",),
            vmem_limit_bytes=VMEM_LIMIT),
        name="ffn",
    )(h, n2, w1, w2, fn)


def _rope_tables(pos):
    half = RET_DK // 2
    freqs = ROPE_BASE ** (-np.arange(0, half, dtype=np.float64) / half)
    ang = np.asarray(pos, np.float64)[:, None] * freqs[None, :]
    cos, sin = np.cos(ang), np.sin(ang)
    return np.stack([np.concatenate([cos, cos], axis=-1),
                     np.concatenate([-sin, sin], axis=-1)]).astype(np.float32)


def _decay_tables():
    scale = RET_DK ** -0.5
    log_gamma = np.log1p(-np.exp2(-5.0 - np.arange(RET_HEADS, dtype=np.float64)))
    idx = np.arange(CHUNK, dtype=np.float64)
    dmask = scale * np.exp(log_gamma[:, None, None]
                           * np.abs(idx[:, None] - idx[None, :])[None])
    dq = scale * np.exp((idx + 1.0)[:, None] * log_gamma[None, :])
    dk = np.exp((CHUNK - 1.0 - idx)[:, None] * log_gamma[None, :])
    cdec = np.exp(CHUNK * log_gamma)[:, None]
    rep = lambda v: np.repeat(v, RET_DK, axis=-1).astype(np.float32)
    return rep(dq), rep(dk), dmask.astype(np.float32), rep(cdec)


def _lane_rep(v, n):
    return jnp.broadcast_to(v[..., None], v.shape + (n,)).reshape(
        v.shape[:-1] + (v.shape[-1] * n,))


def kernel(x, meta_tokens, norm1_w, w_in, ret_norm_w, conv_w, conv_b, dt_bias,
           a_log, d_skip, ssd_norm_w, w_out, norm2_w, w_ff1, w_ff2, final_norm_w):
    b, seq, d = x.shape
    assert d == D_MODEL and w_in.shape[0] == 1, "single-layer block only"
    tt = min(MIXER_TILE, seq)
    assert seq % tt == 0 and tt % CHUNK == 0

    w = w_in[0]
    w_main = w.astype(BF16)
    lane_pad = lambda v: jnp.pad(v, ((0, 0), (0, 128 - SSD_HEADS)))
    w_dt = lane_pad(w[:, C_DT:]).astype(BF16)
    row2 = lambda v: v.reshape(1, -1).astype(F32)
    hpad = lambda v: lane_pad(row2(v))
    hrow = lambda v: _lane_rep(row2(v), SSD_HEAD_DIM)

    dq, dk, dmask, cdec = _decay_tables()

    consts = [row2(norm1_w[0]), w_main, w_dt, row2(ret_norm_w[0]), dq, dk, dmask,
              cdec, conv_w[0].astype(F32), row2(conv_b[0]), hpad(dt_bias[0]),
              hpad(a_log[0]), hrow(d_skip[0]), row2(ssd_norm_w[0]),
              w_out[0].astype(BF16)]

    m_idx = np.arange(CHUNK)
    x_meta = jnp.concatenate(
        [jnp.zeros((PAD, D_MODEL), x.dtype), meta_tokens.astype(x.dtype)])[None]
    valid = np.broadcast_to((m_idx >= PAD).astype(np.float32)[:, None], (CHUNK, 128))
    rope_meta = _rope_tables(m_idx - PAD)
    _, s0, h0, c0 = _mixer_call(x_meta, rope_meta, valid, consts, None,
                                tt=CHUNK, emit_state=True)

    rope = _rope_tables(np.arange(seq) + N_META)
    (h1,) = _mixer_call(x, rope, None, consts, (s0, h0, c0),
                        tt=tt, emit_state=False)

    tm = min(FFN_TILE, b * seq)
    out = _ffn_call(h1.reshape(b * seq, D_MODEL), row2(norm2_w[0]),
                    w_ff1[0].astype(BF16), w_ff2[0].astype(BF16),
                    row2(final_norm_w), tm=tm)
    return out.reshape(b, seq, D_MODEL)
```

```python
import functools

import jax
import jax.numpy as jnp
import numpy as np
from jax import lax
from jax.experimental import pallas as pl
from jax.experimental.pallas import tpu as pltpu

F32 = jnp.float32
BF16 = jnp.bfloat16

D_MODEL = 1024
CHUNK = 64
N_META = 16
PAD = CHUNK - N_META
EPS = 1e-6
LOG2_E = 1.4426950408889634
ROPE_BASE = 10000.0

RET_HEADS = 8
RET_DK = 128
RET_W = RET_HEADS * RET_DK

SSD_INNER = 1024
SSD_HEAD_DIM = 64
SSD_HEADS = SSD_INNER // SSD_HEAD_DIM
SSD_GROUPS = 4
SSD_HPG = SSD_HEADS // SSD_GROUPS
SSD_STATE = 128
SSD_CONV = 4
SSD_BC = SSD_GROUPS * SSD_STATE
SSD_CONV_DIM = SSD_INNER + 2 * SSD_BC
SSD_GW = SSD_HPG * SSD_HEAD_DIM

MIX_WIDTH = RET_W + SSD_INNER
D_FF = 4 * D_MODEL

C_Q, C_K, C_V, C_G, C_Z, C_XBC, C_DT = 0, 1024, 2048, 3072, 4096, 5120, 7168

HIST = 8

MIXER_TILE = 256
FFN_TILE = 1024
FFN_SLAB = 1024
VMEM_LIMIT = 56 * 1024 * 1024


def _silu(v):
    h = 0.5 * v
    return h + h * jnp.tanh(h)


def _softplus(v):
    return jnp.maximum(v, 0.0) + jnp.log(1.0 + jnp.exp(-jnp.abs(v)))


def _dot(a, b):
    return jnp.dot(a, b, preferred_element_type=F32)


def _dot_nt(a, b):
    return lax.dot_general(a, b, (((1,), (1,)), ((), ())),
                           preferred_element_type=F32)


def _dot_tn(a, b):
    return lax.dot_general(a, b, (((0,), (0,)), ((), ())),
                           preferred_element_type=F32)


def _mixer_kernel(*refs, tt, masked, has_init, emit_state):
    it = iter(refs)
    x_ref = next(it)
    rope_ref = next(it)
    valid_ref = next(it) if masked else None
    n1_ref = next(it)
    wm_ref = next(it)
    wdt_ref = next(it)
    retw_ref = next(it)
    dq_ref = next(it)
    dk_ref = next(it)
    dmask_ref = next(it)
    cdec_ref = next(it)
    convw_ref = next(it)
    convb_ref = next(it)
    dtb_ref = next(it)
    alog_ref = next(it)
    dsk_ref = next(it)
    ssdw_ref = next(it)
    wout_ref = next(it)
    if has_init:
        s0_ref, h0_ref, c0_ref = next(it), next(it), next(it)
    out_ref = next(it)
    if emit_state:
        so_ref, ho_ref, co_ref = next(it), next(it), next(it)
    (qr_s, kr_s, qd_s, kd_s, v_s, gate_s, zs_s, xs_s, dt_s, b_s, c_s,
     xbuf, y_s, sret, hssd) = it

    t = pl.program_id(1)

    @pl.when(t == 0)
    def _init():
        if has_init:
            sret[...] = s0_ref[...]
            hssd[...] = h0_ref[...]
            xbuf[0:HIST, :] = c0_ref[...]
        else:
            sret[...] = jnp.zeros_like(sret)
            hssd[...] = jnp.zeros_like(hssd)
            xbuf[0:HIST, :] = jnp.zeros((HIST, SSD_CONV_DIM), F32)

    n_chunks = tt // CHUNK
    chunks = [slice(c * CHUNK, (c + 1) * CHUNK) for c in range(n_chunks)]
    heads = [slice(RET_DK * h, RET_DK * (h + 1)) for h in range(RET_HEADS)]
    glanes = [slice(SSD_GW * g, SSD_GW * (g + 1)) for g in range(SSD_GROUPS)]
    nlanes = [slice(SSD_STATE * g, SSD_STATE * (g + 1)) for g in range(SSD_GROUPS)]

    row = lax.broadcasted_iota(jnp.int32, (CHUNK, SSD_INNER), 0)
    col = lax.broadcasted_iota(jnp.int32, (CHUNK, SSD_INNER), 1) & (CHUNK - 1)
    tri_r = lax.broadcasted_iota(jnp.int32, (tt, tt), 0)
    tri_c = lax.broadcasted_iota(jnp.int32, (tt, tt), 1)
    tril = jnp.logical_and(tri_c <= tri_r,
                           tri_c // CHUNK == tri_r // CHUNK).astype(BF16)
    blk_r = lax.broadcasted_iota(jnp.int32, (SSD_GW, SSD_GW), 0) // SSD_HEAD_DIM
    blk_c = lax.broadcasted_iota(jnp.int32, (SSD_GW, SSD_GW), 1) // SSD_HEAD_DIM
    a_neg = -jnp.exp(alog_ref[...]) * LOG2_E
    lane = lax.broadcasted_iota(jnp.int32, (tt, 128), 1)
    dq_t = jnp.concatenate([dq_ref[...]] * n_chunks, axis=0)
    dk_t = jnp.concatenate([dk_ref[...]] * n_chunks, axis=0)
    st = {}

    def head_lanes(v):
        cols = []
        for j in range(SSD_HEADS // 2):
            lo = jnp.broadcast_to(v[:, 2 * j:2 * j + 1], (tt, 128))
            hi = jnp.broadcast_to(v[:, 2 * j + 1:2 * j + 2], (tt, 128))
            cols.append(jnp.where(lane < SSD_HEAD_DIM, lo, hi))
        return jnp.concatenate(cols, axis=1)

    def p_norm():
        x = x_ref[0]
        ms = jnp.mean(x * x, axis=-1, keepdims=True)
        st["x"] = x
        st["xn"] = ((x * lax.rsqrt(ms + EPS)) * n1_ref[...]).astype(BF16)

    def p_dt():
        dt_c = _softplus(_dot(st["xn"], wdt_ref[...]) + dtb_ref[...])
        if masked:
            dt_c = dt_c * valid_ref[:, 0:1]
        dt_s[...] = head_lanes(dt_c)
        da = dt_c * a_neg
        da_hi = da.astype(BF16)
        st["da_hi"] = da_hi
        st["da_mid"] = (da - da_hi.astype(F32)).astype(BF16)

    def conv_silu(lo, hi):
        xbuf[HIST:HIST + tt, lo:hi] = _dot(st["xn"], wm_ref[:, C_XBC + lo:C_XBC + hi])
        conv = (convb_ref[:, lo:hi]
                + convw_ref[SSD_CONV - 1:SSD_CONV, lo:hi] * xbuf[HIST:HIST + tt, lo:hi])
        for j in range(SSD_CONV - 1):
            off = HIST - (SSD_CONV - 1) + j
            conv = conv + convw_ref[j:j + 1, lo:hi] * xbuf[off:off + tt, lo:hi]
        xbuf[0:HIST, lo:hi] = xbuf[tt:tt + HIST, lo:hi]
        act = _silu(conv)
        if masked:
            act = act * valid_ref[:, 0:1]
        return act

    def p_xs():
        xs_s[...] = conv_silu(0, SSD_INNER)

    def p_bc():
        bc = conv_silu(SSD_INNER, SSD_CONV_DIM)
        b_s[...] = bc[:, :SSD_BC].astype(BF16)
        c_s[...] = bc[:, SSD_BC:].astype(BF16)

    def p_q():
        cos2, sin2 = rope_ref[0], rope_ref[1]
        qf = _dot(st["xn"], wm_ref[:, C_Q:C_K])
        for sl in heads:
            qh = qf[:, sl]
            qr = qh * cos2 + pltpu.roll(qh, RET_DK // 2, 1) * sin2
            qr_s[:, sl] = qr.astype(BF16)
            qd_s[:, sl] = (qr * dq_t[:, sl]).astype(BF16)

    def p_k():
        cos2, sin2 = rope_ref[0], rope_ref[1]
        kf = _dot(st["xn"], wm_ref[:, C_K:C_V])
        for sl in heads:
            kh = kf[:, sl]
            kr = kh * cos2 + pltpu.roll(kh, RET_DK // 2, 1) * sin2
            kr_s[:, sl] = kr.astype(BF16)
            kd_s[:, sl] = (kr * dk_t[:, sl]).astype(BF16)

    def p_v():
        v_s[...] = _dot(st["xn"], wm_ref[:, C_V:C_G]).astype(BF16)

    def p_g():
        gate_s[...] = _silu(_dot(st["xn"], wm_ref[:, C_G:C_Z])) * retw_ref[...]

    def p_z():
        zs_s[...] = _silu(_dot(st["xn"], wm_ref[:, C_Z:C_XBC]))

    def m_cumsum():
        st["acol"] = head_lanes(_dot(tril, st["da_hi"]) + _dot(tril, st["da_mid"]))

    def m_cb():
        cb = [[None] * SSD_GROUPS for _ in chunks]
        for c, rows in enumerate(chunks):
            for g in range(SSD_GROUPS):
                brep = jnp.concatenate([b_s[rows, nlanes[g]]] * SSD_HPG, axis=0)
                cb[c][g] = _dot_nt(c_s[rows, nlanes[g]], brep)
        st["cb"] = cb

    def m_scores():
        lhs_o = [[None] * RET_HEADS for _ in chunks]
        kv = [[None] * RET_HEADS for _ in chunks]
        for c, rows in enumerate(chunks):
            for h in range(RET_HEADS):
                sl = heads[h]
                sc = _dot_nt(qr_s[rows, sl], kr_s[rows, sl])
                p = (sc * dmask_ref[h]).astype(BF16)
                lhs_o[c][h] = jnp.concatenate([qd_s[rows, sl], p], axis=1)
                kv[c][h] = _dot_tn(kd_s[rows, sl], v_s[rows, sl])
        st["lhs_o"], st["kv"] = lhs_o, kv

    def m_intra():
        ydiag = [[None] * SSD_GROUPS for _ in chunks]
        dstate = [[None] * SSD_GROUPS for _ in chunks]
        exp_a, exp_last = [], []
        for c, rows in enumerate(chunks):
            acol = st["acol"][rows]
            arow = jnp.sum(jnp.where(row == col, acol, 0.0), axis=0, keepdims=True)
            lmat = jnp.exp2(jnp.where(col <= row, acol - arow, -1e30))
            a_last = acol[CHUNK - 1:CHUNK, :]
            exp_a.append(jnp.exp2(acol))
            exp_last.append(jnp.exp2(a_last))
            xdt = xs_s[rows, :] * dt_s[rows, :]
            xw = (xdt * jnp.exp2(a_last - acol)).astype(BF16)
            xdt_b = xdt.astype(BF16)
            for g in range(SSD_GROUPS):
                wgt = (st["cb"][c][g] * lmat[:, glanes[g]]).astype(BF16)
                xrep = jnp.concatenate([xdt_b[:, glanes[g]]] * SSD_HPG, axis=0)
                bd = jnp.where(blk_r == blk_c, xrep, jnp.zeros_like(xrep))
                ydiag[c][g] = _dot(wgt, bd)
                dstate[c][g] = _dot_tn(b_s[rows, nlanes[g]], xw[:, glanes[g]])
        st["ydiag"], st["dstate"] = ydiag, dstate
        st["exp_a"], st["exp_last"] = exp_a, exp_last

    def m_states():
        st_in = [[None] * RET_HEADS for _ in chunks]
        for h in range(RET_HEADS):
            s = sret[h]
            for c in range(n_chunks):
                st_in[c][h] = s.astype(BF16)
                s = cdec_ref[h:h + 1, :] * s + st["kv"][c][h]
            sret[h] = s
        hg_in = [[None] * SSD_GROUPS for _ in chunks]
        for g in range(SSD_GROUPS):
            hg = hssd[g]
            for c in range(n_chunks):
                hg_in[c][g] = hg.astype(BF16)
                hg = st["exp_last"][c][:, glanes[g]] * hg + st["dstate"][c][g]
            hssd[g] = hg
        st["st_in"], st["hg_in"] = st_in, hg_in

    def m_ret_out():
        for c, rows in enumerate(chunks):
            for h in range(RET_HEADS):
                o = _dot(st["lhs_o"][c][h],
                         jnp.concatenate([st["st_in"][c][h], v_s[rows, heads[h]]], axis=0))
                o = o * lax.rsqrt(jnp.mean(o * o, axis=-1, keepdims=True) + EPS)
                y_s[rows, heads[h]] = (o * gate_s[rows, heads[h]]).astype(BF16)

    def m_proj_ret():
        st["acc"] = st["x"] + _dot(y_s[:, :RET_W], wout_ref[:RET_W, :])

    def m_ssd_out():
        for c, rows in enumerate(chunks):
            for g in range(SSD_GROUPS):
                gl = glanes[g]
                yg = (st["ydiag"][c][g]
                      + _dot(c_s[rows, nlanes[g]], st["hg_in"][c][g]) * st["exp_a"][c][:, gl]
                      + xs_s[rows, gl] * dsk_ref[:, gl])
                yg = yg * zs_s[rows, gl]
                yg = yg * lax.rsqrt(jnp.mean(yg * yg, axis=-1, keepdims=True) + EPS)
                y_s[rows, RET_W + SSD_GW * g:RET_W + SSD_GW * (g + 1)] = (
                    yg * ssdw_ref[:, gl]).astype(BF16)

    def m_proj_ssd():
        out_ref[0] = st["acc"] + _dot(y_s[:, RET_W:], wout_ref[RET_W:, :])

    for stage in (p_norm, p_dt, p_xs, p_bc, p_q, p_k, m_cumsum, p_v, m_cb, p_g,
                  m_scores, p_z, m_intra, m_states, m_ret_out, m_proj_ret,
                  m_ssd_out, m_proj_ssd):
        stage()

    if emit_state:
        so_ref[...] = sret[...]
        ho_ref[...] = hssd[...]
        co_ref[...] = xbuf[0:HIST, :]


def _const_spec(shape, single=True):
    nd = len(shape)
    kw = {"pipeline_mode": pl.Buffered(1)} if single else {}
    return pl.BlockSpec(shape, lambda *_: (0,) * nd, **kw)


def _mixer_call(x, rope, valid, consts, init, *, tt, emit_state):
    b, n, _ = x.shape
    masked = valid is not None
    has_init = init is not None
    nt = n // tt

    in_specs = [pl.BlockSpec((1, tt, D_MODEL), lambda i, j: (i, j, 0)),
                pl.BlockSpec((2, tt, RET_DK), lambda i, j: (0, j, 0))]
    args = [x, rope]
    if masked:
        in_specs.append(pl.BlockSpec((tt, 128), lambda i, j: (j, 0)))
        args.append(valid)
    for c in consts:
        in_specs.append(_const_spec(c.shape))
        args.append(c)
    if has_init:
        for c in init:
            in_specs.append(_const_spec(c.shape))
            args.append(c)

    out_shape = [jax.ShapeDtypeStruct((b, n, D_MODEL), F32)]
    out_specs = [pl.BlockSpec((1, tt, D_MODEL), lambda i, j: (i, j, 0))]
    if emit_state:
        st_shapes = [(RET_HEADS, RET_DK, RET_DK), (SSD_GROUPS, SSD_STATE, SSD_GW),
                     (HIST, SSD_CONV_DIM)]
        for s in st_shapes:
            out_shape.append(jax.ShapeDtypeStruct(s, F32))
            out_specs.append(_const_spec(s, single=False))

    scratch = [pltpu.VMEM((tt, RET_W), BF16)] * 5
    scratch += [pltpu.VMEM((tt, SSD_INNER), F32)] * 4
    scratch += [pltpu.VMEM((tt, SSD_BC), BF16)] * 2
    scratch += [pltpu.VMEM((tt + HIST, SSD_CONV_DIM), F32),
                pltpu.VMEM((tt, MIX_WIDTH), BF16),
                pltpu.VMEM((RET_HEADS, RET_DK, RET_DK), F32),
                pltpu.VMEM((SSD_GROUPS, SSD_STATE, SSD_GW), F32)]

    kern = functools.partial(_mixer_kernel, tt=tt, masked=masked,
                             has_init=has_init, emit_state=emit_state)
    return pl.pallas_call(
        kern,
        grid=(b, nt),
        in_specs=in_specs,
        out_specs=out_specs,
        out_shape=out_shape,
        scratch_shapes=scratch,
        compiler_params=pltpu.CompilerParams(
            dimension_semantics=("arbitrary", "arbitrary"),
            vmem_limit_bytes=VMEM_LIMIT),
        name="mixer_meta" if emit_state else "mixer",
    )(*args)


def _ffn_kernel(h_ref, n2_ref, w1_ref, w2_ref, fn_ref, out_ref):
    h = h_ref[...]
    ms = jnp.mean(h * h, axis=-1, keepdims=True)
    u = ((h * lax.rsqrt(ms + EPS)) * n2_ref[...]).astype(BF16)
    acc = h
    for j in range(D_FF // FFN_SLAB):
        sl = slice(FFN_SLAB * j, FFN_SLAB * (j + 1))
        a = jnp.maximum(_dot(u, w1_ref[:, sl]), 0.0)
        acc = acc + _dot((a * a).astype(BF16), w2_ref[sl, :])
    ms2 = jnp.mean(acc * acc, axis=-1, keepdims=True)
    out_ref[...] = (acc * lax.rsqrt(ms2 + EPS)) * fn_ref[...]


def _ffn_call(h, n2, w1, w2, fn, *, tm):
    n = h.shape[0]
    return pl.pallas_call(
        _ffn_kernel,
        grid=(n // tm,),
        in_specs=[pl.BlockSpec((tm, D_MODEL), lambda i: (i, 0)),
                  _const_spec(n2.shape), _const_spec(w1.shape),
                  _const_spec(w2.shape), _const_spec(fn.shape)],
        out_specs=pl.BlockSpec((tm, D_MODEL), lambda i: (i, 0)),
        out_shape=jax.ShapeDtypeStruct((n, D_MODEL), F32),
        compiler_params=pltpu.CompilerParams(
            dimension_semantics=("arbitrary",),
            vmem_limit_bytes=VMEM_LIMIT),
        name="ffn",
    )(h, n2, w1, w2, fn)


def _rope_tables(pos):
    half = RET_DK // 2
    freqs = ROPE_BASE ** (-np.arange(0, half, dtype=np.float64) / half)
    ang = np.asarray(pos, np.float64)[:, None] * freqs[None, :]
    cos, sin = np.cos(ang), np.sin(ang)
    return np.stack([np.concatenate([cos, cos], axis=-1),
                     np.concatenate([-sin, sin], axis=-1)]).astype(np.float32)


def _decay_tables():
    scale = RET_DK ** -0.5
    log_gamma = np.log1p(-np.exp2(-5.0 - np.arange(RET_HEADS, dtype=np.float64)))
    idx = np.arange(CHUNK, dtype=np.float64)
    dmask = scale * np.exp(log_gamma[:, None, None]
                           * np.abs(idx[:, None] - idx[None, :])[None])
    dq = scale * np.exp((idx + 1.0)[:, None] * log_gamma[None, :])
    dk = np.exp((CHUNK - 1.0 - idx)[:, None] * log_gamma[None, :])
    cdec = np.exp(CHUNK * log_gamma)[:, None]
    rep = lambda v: np.repeat(v, RET_DK, axis=-1).astype(np.float32)
    return rep(dq), rep(dk), dmask.astype(np.float32), rep(cdec)


def _lane_rep(v, n):
    return jnp.broadcast_to(v[..., None], v.shape + (n,)).reshape(
        v.shape[:-1] + (v.shape[-1] * n,))


def kernel(x, meta_tokens, norm1_w, w_in, ret_norm_w, conv_w, conv_b, dt_bias,
           a_log, d_skip, ssd_norm_w, w_out, norm2_w, w_ff1, w_ff2, final_norm_w):
    b, seq, d = x.shape
    assert d == D_MODEL and w_in.shape[0] == 1, "single-layer block only"
    tt = min(MIXER_TILE, seq)
    assert seq % tt == 0 and tt % CHUNK == 0

    w = w_in[0]
    w_main = w.astype(BF16)
    lane_pad = lambda v: jnp.pad(v, ((0, 0), (0, 128 - SSD_HEADS)))
    w_dt = lane_pad(w[:, C_DT:]).astype(BF16)
    row2 = lambda v: v.reshape(1, -1).astype(F32)
    hpad = lambda v: lane_pad(row2(v))
    hrow = lambda v: _lane_rep(row2(v), SSD_HEAD_DIM)

    dq, dk, dmask, cdec = _decay_tables()

    consts = [row2(norm1_w[0]), w_main, w_dt, row2(ret_norm_w[0]), dq, dk, dmask,
              cdec, conv_w[0].astype(F32), row2(conv_b[0]), hpad(dt_bias[0]),
              hpad(a_log[0]), hrow(d_skip[0]), row2(ssd_norm_w[0]),
              w_out[0].astype(BF16)]

    m_idx = np.arange(CHUNK)
    x_meta = jnp.concatenate(
        [jnp.zeros((PAD, D_MODEL), x.dtype), meta_tokens.astype(x.dtype)])[None]
    valid = np.broadcast_to((m_idx >= PAD).astype(np.float32)[:, None], (CHUNK, 128))
    rope_meta = _rope_tables(m_idx - PAD)
    _, s0, h0, c0 = _mixer_call(x_meta, rope_meta, valid, consts, None,
                                tt=CHUNK, emit_state=True)

    rope = _rope_tables(np.arange(seq) + N_META)
    (h1,) = _mixer_call(x, rope, None, consts, (s0, h0, c0),
                        tt=tt, emit_state=False)

    tm = min(FFN_TILE, b * seq)
    out = _ffn_call(h1.reshape(b * seq, D_MODEL), row2(norm2_w[0]),
                    w_ff1[0].astype(BF16), w_ff2[0].astype(BF16),
                    row2(final_norm_w), tm=tm)
    return out.reshape(b, seq, D_MODEL)
```

```python
import functools
import math

import jax
import jax.numpy as jnp
import numpy as np
from jax import lax
from jax.experimental import pallas as pl
from jax.experimental.pallas import tpu as pltpu

F32 = jnp.float32
BF16 = jnp.bfloat16

D_MODEL = 1024
CHUNK = 64
RET_BLOCK = 128
N_META = 16
PAD = CHUNK - N_META
EPS = 1e-6
ROPE_BASE = 10000.0

RET_HEADS = 8
RET_DK = 128
RET_W = RET_HEADS * RET_DK

SSD_INNER = 1024
SSD_HEAD_DIM = 64
SSD_HEADS = SSD_INNER // SSD_HEAD_DIM
SSD_GROUPS = 4
SSD_HPG = SSD_HEADS // SSD_GROUPS
SSD_STATE = 128
SSD_CONV = 4
SSD_BC = SSD_GROUPS * SSD_STATE
SSD_CONV_DIM = SSD_INNER + 2 * SSD_BC
SSD_GW = SSD_HPG * SSD_HEAD_DIM

MIX_WIDTH = RET_W + SSD_INNER
D_FF = 4 * D_MODEL

C_Q, C_K, C_V, C_G, C_Z, C_XBC, C_DT = 0, 1024, 2048, 3072, 4096, 5120, 7168

HIST = 8

MIXER_TILE = 256
FFN_TILE = 1024
FFN_SLAB = 1024
VMEM_LIMIT = 56 * 1024 * 1024


def _silu(v):
    h = 0.5 * v
    return h + h * jnp.tanh(h)


def _softplus(v):
    return jnp.maximum(v, 0.0) + jnp.log(1.0 + jnp.exp(-jnp.abs(v)))


def _dot(a, b):
    return jnp.dot(a, b, preferred_element_type=F32)


def _dot_nt(a, b):
    return lax.dot_general(a, b, (((1,), (1,)), ((), ())),
                           preferred_element_type=F32)


def _dot_tn(a, b):
    return lax.dot_general(a, b, (((0,), (0,)), ((), ())),
                           preferred_element_type=F32)


def _mixer_kernel(*refs, tt, masked, has_init, emit_state):
    it = iter(refs)
    x_ref = next(it)
    rope_ref = next(it)
    valid_ref = next(it) if masked else None
    n1_ref = next(it)
    wm_ref = next(it)
    wdt_ref = next(it)
    retw_ref = next(it)
    dq_ref = next(it)
    dk_ref = next(it)
    dmask_ref = next(it)
    cdec_ref = next(it)
    convw_ref = next(it)
    convb_ref = next(it)
    dtb_ref = next(it)
    alog_ref = next(it)
    dsk_ref = next(it)
    ssdw_ref = next(it)
    wout_ref = next(it)
    if has_init:
        s0_ref, h0_ref, c0_ref = next(it), next(it), next(it)
    out_ref = next(it)
    if emit_state:
        so_ref, ho_ref, co_ref = next(it), next(it), next(it)
    (qr_s, kr_s, qd_s, kd_s, v_s, gate_s, zs_s, xs_s, dt_s, b_s, c_s,
     xbuf, y_s, sret, hssd) = it

    t = pl.program_id(1)

    @pl.when(t == 0)
    def _init():
        if has_init:
            sret[...] = s0_ref[...]
            hssd[...] = h0_ref[...]
            xbuf[0:HIST, :] = c0_ref[...]
        else:
            sret[...] = jnp.zeros_like(sret)
            hssd[...] = jnp.zeros_like(hssd)
            xbuf[0:HIST, :] = jnp.zeros((HIST, SSD_CONV_DIM), F32)

    n_chunks = tt // CHUNK
    rb = dq_ref.shape[0]
    rblocks = [slice(r * rb, (r + 1) * rb) for r in range(tt // rb)]
    chunks = [slice(c * CHUNK, (c + 1) * CHUNK) for c in range(n_chunks)]
    heads = [slice(RET_DK * h, RET_DK * (h + 1)) for h in range(RET_HEADS)]
    glanes = [slice(SSD_GW * g, SSD_GW * (g + 1)) for g in range(SSD_GROUPS)]
    nlanes = [slice(SSD_STATE * g, SSD_STATE * (g + 1)) for g in range(SSD_GROUPS)]

    row = lax.broadcasted_iota(jnp.int32, (CHUNK, SSD_INNER), 0)
    col = lax.broadcasted_iota(jnp.int32, (CHUNK, SSD_INNER), 1) & (CHUNK - 1)
    tri_r = lax.broadcasted_iota(jnp.int32, (tt, tt), 0)
    tri_c = lax.broadcasted_iota(jnp.int32, (tt, tt), 1)
    tril = jnp.logical_and(tri_c <= tri_r,
                           tri_c // CHUNK == tri_r // CHUNK).astype(BF16)
    blk_r = lax.broadcasted_iota(jnp.int32, (SSD_GW, SSD_GW), 0) // SSD_HEAD_DIM
    blk_c = lax.broadcasted_iota(jnp.int32, (SSD_GW, SSD_GW), 1) // SSD_HEAD_DIM
    a_neg = -jnp.exp(alog_ref[...])
    lane = lax.broadcasted_iota(jnp.int32, (tt, 128), 1)
    dq_t = jnp.concatenate([dq_ref[...]] * len(rblocks), axis=0)
    dk_t = jnp.concatenate([dk_ref[...]] * len(rblocks), axis=0)
    st = {}

    def head_lanes(v):
        cols = []
        for j in range(SSD_HEADS // 2):
            lo = jnp.broadcast_to(v[:, 2 * j:2 * j + 1], (tt, 128))
            hi = jnp.broadcast_to(v[:, 2 * j + 1:2 * j + 2], (tt, 128))
            cols.append(jnp.where(lane < SSD_HEAD_DIM, lo, hi))
        return jnp.concatenate(cols, axis=1)

    def p_norm():
        x = x_ref[0]
        ms = jnp.mean(x * x, axis=-1, keepdims=True)
        st["x"] = x
        st["xn"] = ((x * lax.rsqrt(ms + EPS)) * n1_ref[...]).astype(BF16)

    def p_dt():
        dt_c = _softplus(_dot(st["xn"], wdt_ref[...]) + dtb_ref[...])
        if masked:
            dt_c = dt_c * valid_ref[:, 0:1]
        dt_s[...] = head_lanes(dt_c)
        da = dt_c * a_neg
        da_hi = da.astype(BF16)
        st["da_hi"] = da_hi
        st["da_mid"] = (da - da_hi.astype(F32)).astype(BF16)

    def conv_silu(lo, hi):
        xbuf[HIST:HIST + tt, lo:hi] = _dot(st["xn"], wm_ref[:, C_XBC + lo:C_XBC + hi])
        conv = (convb_ref[:, lo:hi]
                + convw_ref[SSD_CONV - 1:SSD_CONV, lo:hi] * xbuf[HIST:HIST + tt, lo:hi])
        for j in range(SSD_CONV - 1):
            off = HIST - (SSD_CONV - 1) + j
            conv = conv + convw_ref[j:j + 1, lo:hi] * xbuf[off:off + tt, lo:hi]
        xbuf[0:HIST, lo:hi] = xbuf[tt:tt + HIST, lo:hi]
        act = _silu(conv)
        if masked:
            act = act * valid_ref[:, 0:1]
        return act

    def p_xs():
        xs_s[...] = conv_silu(0, SSD_INNER)

    def p_bc():
        bc = conv_silu(SSD_INNER, SSD_CONV_DIM)
        b_s[...] = bc[:, :SSD_BC].astype(BF16)
        c_s[...] = bc[:, SSD_BC:].astype(BF16)

    def p_q():
        cos2, sin2 = rope_ref[0], rope_ref[1]
        qf = _dot(st["xn"], wm_ref[:, C_Q:C_K])
        for sl in heads:
            qh = qf[:, sl]
            qr = qh * cos2 + pltpu.roll(qh, RET_DK // 2, 1) * sin2
            qr_s[:, sl] = qr.astype(BF16)
            qd_s[:, sl] = (qr * dq_t[:, sl]).astype(BF16)

    def p_k():
        cos2, sin2 = rope_ref[0], rope_ref[1]
        kf = _dot(st["xn"], wm_ref[:, C_K:C_V])
        for sl in heads:
            kh = kf[:, sl]
            kr = kh * cos2 + pltpu.roll(kh, RET_DK // 2, 1) * sin2
            kr_s[:, sl] = kr.astype(BF16)
            kd_s[:, sl] = (kr * dk_t[:, sl]).astype(BF16)

    def p_v():
        v_s[...] = _dot(st["xn"], wm_ref[:, C_V:C_G]).astype(BF16)

    def p_g():
        gate_s[...] = _silu(_dot(st["xn"], wm_ref[:, C_G:C_Z])) * retw_ref[...]

    def p_z():
        zs_s[...] = _silu(_dot(st["xn"], wm_ref[:, C_Z:C_XBC]))

    def m_cumsum():
        st["acol"] = head_lanes(_dot(tril, st["da_hi"]) + _dot(tril, st["da_mid"]))

    def m_cb():
        cb = [[None] * SSD_GROUPS for _ in chunks]
        for c, rows in enumerate(chunks):
            for g in range(SSD_GROUPS):
                brep = jnp.concatenate([b_s[rows, nlanes[g]]] * SSD_HPG, axis=0)
                cb[c][g] = _dot_nt(c_s[rows, nlanes[g]], brep)
        st["cb"] = cb

    def m_scores():
        lhs_o = [[None] * RET_HEADS for _ in rblocks]
        kv = [[None] * RET_HEADS for _ in rblocks]
        for c, rows in enumerate(rblocks):
            for h in range(RET_HEADS):
                sl = heads[h]
                sc = _dot_nt(qr_s[rows, sl], kr_s[rows, sl])
                p = (sc * dmask_ref[h]).astype(BF16)
                lhs_o[c][h] = jnp.concatenate([qd_s[rows, sl], p], axis=1)
                kv[c][h] = _dot_tn(kd_s[rows, sl], v_s[rows, sl])
        st["lhs_o"], st["kv"] = lhs_o, kv

    def m_intra():
        ydiag = [[None] * SSD_GROUPS for _ in chunks]
        dstate = [[None] * SSD_GROUPS for _ in chunks]
        exp_a, exp_last = [], []
        for c, rows in enumerate(chunks):
            acol = st["acol"][rows]
            arow = jnp.sum(jnp.where(row == col, acol, 0.0), axis=0, keepdims=True)
            lmat = jnp.exp(jnp.where(col <= row, acol - arow, -1e30))
            a_last = acol[CHUNK - 1:CHUNK, :]
            exp_a.append(jnp.exp(acol))
            exp_last.append(jnp.exp(a_last))
            xdt = xs_s[rows, :] * dt_s[rows, :]
            xw = (xdt * jnp.exp(a_last - acol)).astype(BF16)
            xdt_b = xdt.astype(BF16)
            for g in range(SSD_GROUPS):
                wgt = (st["cb"][c][g] * lmat[:, glanes[g]]).astype(BF16)
                xrep = jnp.concatenate([xdt_b[:, glanes[g]]] * SSD_HPG, axis=0)
                bd = jnp.where(blk_r == blk_c, xrep, jnp.zeros_like(xrep))
                ydiag[c][g] = _dot(wgt, bd)
                dstate[c][g] = _dot_tn(b_s[rows, nlanes[g]], xw[:, glanes[g]])
        st["ydiag"], st["dstate"] = ydiag, dstate
        st["exp_a"], st["exp_last"] = exp_a, exp_last

    def m_states():
        st_in = [[None] * RET_HEADS for _ in rblocks]
        for h in range(RET_HEADS):
            s = sret[h]
            for c in range(len(rblocks)):
                st_in[c][h] = s.astype(BF16)
                s = cdec_ref[h:h + 1, :] * s + st["kv"][c][h]
            sret[h] = s
        hg_in = [[None] * SSD_GROUPS for _ in chunks]
        for g in range(SSD_GROUPS):
            hg = hssd[g]
            for c in range(n_chunks):
                hg_in[c][g] = hg.astype(BF16)
                hg = st["exp_last"][c][:, glanes[g]] * hg + st["dstate"][c][g]
            hssd[g] = hg
        st["st_in"], st["hg_in"] = st_in, hg_in

    def m_ret_out():
        for c, rows in enumerate(rblocks):
            for h in range(RET_HEADS):
                o = _dot(st["lhs_o"][c][h],
                         jnp.concatenate([st["st_in"][c][h], v_s[rows, heads[h]]], axis=0))
                o = o * lax.rsqrt(jnp.mean(o * o, axis=-1, keepdims=True) + EPS)
                y_s[rows, heads[h]] = (o * gate_s[rows, heads[h]]).astype(BF16)

    def m_proj_ret():
        st["acc"] = st["x"] + _dot(y_s[:, :RET_W], wout_ref[:RET_W, :])

    def m_ssd_out():
        for c, rows in enumerate(chunks):
            for g in range(SSD_GROUPS):
                gl = glanes[g]
                yg = (st["ydiag"][c][g]
                      + _dot(c_s[rows, nlanes[g]], st["hg_in"][c][g]) * st["exp_a"][c][:, gl]
                      + xs_s[rows, gl] * dsk_ref[:, gl])
                yg = yg * zs_s[rows, gl]
                yg = yg * lax.rsqrt(jnp.mean(yg * yg, axis=-1, keepdims=True) + EPS)
                y_s[rows, RET_W + SSD_GW * g:RET_W + SSD_GW * (g + 1)] = (
                    yg * ssdw_ref[:, gl]).astype(BF16)

    def m_proj_ssd():
        out_ref[0] = st["acc"] + _dot(y_s[:, RET_W:], wout_ref[RET_W:, :])

    for stage in (p_norm, p_dt, p_xs, p_bc, p_q, p_k, m_cumsum, p_v, m_cb, p_g,
                  m_scores, p_z, m_intra, m_states, m_ret_out, m_proj_ret,
                  m_ssd_out, m_proj_ssd):
        stage()

    if emit_state:
        so_ref[...] = sret[...]
        ho_ref[...] = hssd[...]
        co_ref[...] = xbuf[0:HIST, :]


def _const_spec(shape, single=True):
    nd = len(shape)
    kw = {"pipeline_mode": pl.Buffered(1)} if single else {}
    return pl.BlockSpec(shape, lambda *_: (0,) * nd, **kw)


def _mixer_call(x, rope, valid, consts, init, *, tt, emit_state):
    b, n, _ = x.shape
    masked = valid is not None
    has_init = init is not None
    nt = n // tt

    in_specs = [pl.BlockSpec((1, tt, D_MODEL), lambda i, j: (i, j, 0)),
                pl.BlockSpec((2, tt, RET_DK), lambda i, j: (0, j, 0))]
    args = [x, rope]
    if masked:
        in_specs.append(pl.BlockSpec((tt, 128), lambda i, j: (j, 0)))
        args.append(valid)
    for c in consts:
        in_specs.append(_const_spec(c.shape))
        args.append(c)
    if has_init:
        for c in init:
            in_specs.append(_const_spec(c.shape))
            args.append(c)

    out_shape = [jax.ShapeDtypeStruct((b, n, D_MODEL), F32)]
    out_specs = [pl.BlockSpec((1, tt, D_MODEL), lambda i, j: (i, j, 0))]
    if emit_state:
        st_shapes = [(RET_HEADS, RET_DK, RET_DK), (SSD_GROUPS, SSD_STATE, SSD_GW),
                     (HIST, SSD_CONV_DIM)]
        for s in st_shapes:
            out_shape.append(jax.ShapeDtypeStruct(s, F32))
            out_specs.append(_const_spec(s, single=False))

    scratch = [pltpu.VMEM((tt, RET_W), BF16)] * 5
    scratch += [pltpu.VMEM((tt, SSD_INNER), F32)] * 4
    scratch += [pltpu.VMEM((tt, SSD_BC), BF16)] * 2
    scratch += [pltpu.VMEM((tt + HIST, SSD_CONV_DIM), F32),
                pltpu.VMEM((tt, MIX_WIDTH), BF16),
                pltpu.VMEM((RET_HEADS, RET_DK, RET_DK), F32),
                pltpu.VMEM((SSD_GROUPS, SSD_STATE, SSD_GW), F32)]

    kern = functools.partial(_mixer_kernel, tt=tt, masked=masked,
                             has_init=has_init, emit_state=emit_state)
    return pl.pallas_call(
        kern,
        grid=(b, nt),
        in_specs=in_specs,
        out_specs=out_specs,
        out_shape=out_shape,
        scratch_shapes=scratch,
        compiler_params=pltpu.CompilerParams(
            dimension_semantics=("arbitrary", "arbitrary"),
            vmem_limit_bytes=VMEM_LIMIT),
        name="mixer_meta" if emit_state else "mixer",
    )(*args)


def _ffn_kernel(h_ref, n2_ref, w1_ref, w2_ref, fn_ref, out_ref):
    h = h_ref[...]
    ms = jnp.mean(h * h, axis=-1, keepdims=True)
    u = ((h * lax.rsqrt(ms + EPS)) * n2_ref[...]).astype(BF16)
    acc = h
    for j in range(D_FF // FFN_SLAB):
        sl = slice(FFN_SLAB * j, FFN_SLAB * (j + 1))
        a = jnp.maximum(_dot(u, w1_ref[:, sl]), 0.0)
        acc = acc + _dot((a * a).astype(BF16), w2_ref[sl, :])
    ms2 = jnp.mean(acc * acc, axis=-1, keepdims=True)
    out_ref[...] = (acc * lax.rsqrt(ms2 + EPS)) * fn_ref[...]


def _ffn_call(h, n2, w1, w2, fn, *, tm):
    n = h.shape[0]
    return pl.pallas_call(
        _ffn_kernel,
        grid=(n // tm,),
        in_specs=[pl.BlockSpec((tm, D_MODEL), lambda i: (i, 0)),
                  _const_spec(n2.shape), _const_spec(w1.shape),
                  _const_spec(w2.shape), _const_spec(fn.shape)],
        out_specs=pl.BlockSpec((tm, D_MODEL), lambda i: (i, 0)),
        out_shape=jax.ShapeDtypeStruct((n, D_MODEL), F32),
        compiler_params=pltpu.CompilerParams(
            dimension_semantics=("arbitrary",),
            vmem_limit_bytes=VMEM_LIMIT),
        name="ffn",
    )(h, n2, w1, w2, fn)


def _rope_tables(pos):
    half = RET_DK // 2
    freqs = ROPE_BASE ** (-np.arange(0, half, dtype=np.float64) / half)
    ang = np.asarray(pos, np.float64)[:, None] * freqs[None, :]
    cos, sin = np.cos(ang), np.sin(ang)
    return np.stack([np.concatenate([cos, cos], axis=-1),
                     np.concatenate([-sin, sin], axis=-1)]).astype(np.float32)


def _decay_tables(rb):
    scale = RET_DK ** -0.5
    log_gamma = np.log1p(-np.exp2(-5.0 - np.arange(RET_HEADS, dtype=np.float64)))
    idx = np.arange(rb, dtype=np.float64)
    later = (idx[None, :] // CHUNK) > (idx[:, None] // CHUNK)
    dmask = scale * np.exp(log_gamma[:, None, None]
                           * np.abs(idx[:, None] - idx[None, :])[None])
    dmask = np.where(later[None], 0.0, dmask)
    dq = scale * np.exp((idx + 1.0)[:, None] * log_gamma[None, :])
    dk = np.exp((rb - 1.0 - idx)[:, None] * log_gamma[None, :])
    cdec = np.exp(rb * log_gamma)[:, None]
    rep = lambda v: np.repeat(v, RET_DK, axis=-1).astype(np.float32)
    return rep(dq), rep(dk), dmask.astype(np.float32), rep(cdec)


def _lane_rep(v, n):
    return jnp.broadcast_to(v[..., None], v.shape + (n,)).reshape(
        v.shape[:-1] + (v.shape[-1] * n,))


def kernel(x, meta_tokens, norm1_w, w_in, ret_norm_w, conv_w, conv_b, dt_bias,
           a_log, d_skip, ssd_norm_w, w_out, norm2_w, w_ff1, w_ff2, final_norm_w):
    b, seq, d = x.shape
    assert d == D_MODEL and w_in.shape[0] == 1, "single-layer block only"
    tt = min(MIXER_TILE, seq)
    assert seq % tt == 0 and tt % CHUNK == 0

    w = w_in[0]
    w_main = w.astype(BF16)
    lane_pad = lambda v: jnp.pad(v, ((0, 0), (0, 128 - SSD_HEADS)))
    w_dt = lane_pad(w[:, C_DT:]).astype(BF16)
    row2 = lambda v: v.reshape(1, -1).astype(F32)
    hpad = lambda v: lane_pad(row2(v))
    hrow = lambda v: _lane_rep(row2(v), SSD_HEAD_DIM)

    def consts(rb):
        dq, dk, dmask, cdec = _decay_tables(rb)
        return [row2(norm1_w[0]), w_main, w_dt, row2(ret_norm_w[0]), dq, dk, dmask,
                cdec, conv_w[0].astype(F32), row2(conv_b[0]), hpad(dt_bias[0]),
                hpad(a_log[0]), hrow(d_skip[0]), row2(ssd_norm_w[0]),
                w_out[0].astype(BF16)]

    m_idx = np.arange(CHUNK)
    x_meta = jnp.concatenate(
        [jnp.zeros((PAD, D_MODEL), x.dtype), meta_tokens.astype(x.dtype)])[None]
    valid = np.broadcast_to((m_idx >= PAD).astype(np.float32)[:, None], (CHUNK, 128))
    rope_meta = _rope_tables(m_idx - PAD)
    _, s0, h0, c0 = _mixer_call(x_meta, rope_meta, valid, consts(CHUNK), None,
                                tt=CHUNK, emit_state=True)

    rope = _rope_tables(np.arange(seq) + N_META)
    (h1,) = _mixer_call(x, rope, None, consts(math.gcd(tt, RET_BLOCK)), (s0, h0, c0),
                        tt=tt, emit_state=False)

    tm = min(FFN_TILE, b * seq)
    out = _ffn_call(h1.reshape(b * seq, D_MODEL), row2(norm2_w[0]),
                    w_ff1[0].astype(BF16), w_ff2[0].astype(BF16),
                    row2(final_norm_w), tm=tm)
    return out.reshape(b, seq, D_MODEL)
```

```python
import functools
import math

import jax
import jax.numpy as jnp
import numpy as np
from jax import lax
from jax.experimental import pallas as pl
from jax.experimental.pallas import tpu as pltpu

F32 = jnp.float32
BF16 = jnp.bfloat16

D_MODEL = 1024
CHUNK = 64
RET_BLOCK = 128
N_META = 16
PAD = CHUNK - N_META
EPS = 1e-6
ROPE_BASE = 10000.0

RET_HEADS = 8
RET_DK = 128
RET_W = RET_HEADS * RET_DK

SSD_INNER = 1024
SSD_HEAD_DIM = 64
SSD_HEADS = SSD_INNER // SSD_HEAD_DIM
SSD_GROUPS = 4
SSD_HPG = SSD_HEADS // SSD_GROUPS
SSD_STATE = 128
SSD_CONV = 4
SSD_BC = SSD_GROUPS * SSD_STATE
SSD_CONV_DIM = SSD_INNER + 2 * SSD_BC
SSD_GW = SSD_HPG * SSD_HEAD_DIM

MIX_WIDTH = RET_W + SSD_INNER
D_FF = 4 * D_MODEL

C_Q, C_K, C_V, C_G, C_Z, C_XBC, C_DT = 0, 1024, 2048, 3072, 4096, 5120, 7168

HIST = 8

MIXER_TILE = 256
FFN_TILE = 1024
FFN_SLAB = 1024
VMEM_LIMIT = 56 * 1024 * 1024


def _silu(v):
    h = 0.5 * v
    return h + h * jnp.tanh(h)


def _softplus(v):
    return jnp.maximum(v, 0.0) + jnp.log(1.0 + jnp.exp(-jnp.abs(v)))


def _dot(a, b):
    return jnp.dot(a, b, preferred_element_type=F32)


def _dot_nt(a, b):
    return lax.dot_general(a, b, (((1,), (1,)), ((), ())),
                           preferred_element_type=F32)


def _dot_tn(a, b):
    return lax.dot_general(a, b, (((0,), (0,)), ((), ())),
                           preferred_element_type=F32)


def _mixer_kernel(*refs, tt, masked, has_init, emit_state):
    it = iter(refs)
    x_ref = next(it)
    rope_ref = next(it)
    valid_ref = next(it) if masked else None
    n1_ref = next(it)
    wm_ref = next(it)
    wdt_ref = next(it)
    retw_ref = next(it)
    dq_ref = next(it)
    dk_ref = next(it)
    dmask_ref = next(it)
    cdec_ref = next(it)
    tril_ref = next(it)
    convw_ref = next(it)
    convb_ref = next(it)
    dtb_ref = next(it)
    alog_ref = next(it)
    dsk_ref = next(it)
    ssdw_ref = next(it)
    wout_ref = next(it)
    if has_init:
        s0_ref, h0_ref, c0_ref = next(it), next(it), next(it)
    out_ref = next(it)
    if emit_state:
        so_ref, ho_ref, co_ref = next(it), next(it), next(it)
    (qr_s, kr_s, qd_s, kd_s, v_s, gate_s, zs_s, xs_s, dt_s, b_s, c_s,
     xbuf, y_s, sret, hssd) = it

    t = pl.program_id(1)

    @pl.when(t == 0)
    def _init():
        if has_init:
            sret[...] = s0_ref[...]
            hssd[...] = h0_ref[...]
            xbuf[0:HIST, :] = c0_ref[...]
        else:
            sret[...] = jnp.zeros_like(sret)
            hssd[...] = jnp.zeros_like(hssd)
            xbuf[0:HIST, :] = jnp.zeros((HIST, SSD_CONV_DIM), F32)

    n_chunks = tt // CHUNK
    rb = dq_ref.shape[0]
    rblocks = [slice(r * rb, (r + 1) * rb) for r in range(tt // rb)]
    chunks = [slice(c * CHUNK, (c + 1) * CHUNK) for c in range(n_chunks)]
    heads = [slice(RET_DK * h, RET_DK * (h + 1)) for h in range(RET_HEADS)]
    glanes = [slice(SSD_GW * g, SSD_GW * (g + 1)) for g in range(SSD_GROUPS)]
    nlanes = [slice(SSD_STATE * g, SSD_STATE * (g + 1)) for g in range(SSD_GROUPS)]

    row = lax.broadcasted_iota(jnp.int32, (CHUNK, SSD_INNER), 0)
    col = lax.broadcasted_iota(jnp.int32, (CHUNK, SSD_INNER), 1) & (CHUNK - 1)
    blk_r = lax.broadcasted_iota(jnp.int32, (SSD_GW, SSD_GW), 0) // SSD_HEAD_DIM
    blk_c = lax.broadcasted_iota(jnp.int32, (SSD_GW, SSD_GW), 1) // SSD_HEAD_DIM
    a_neg = -jnp.exp(alog_ref[...])
    lane = lax.broadcasted_iota(jnp.int32, (tt, 128), 1)
    dq_t = jnp.concatenate([dq_ref[...]] * len(rblocks), axis=0)
    dk_t = jnp.concatenate([dk_ref[...]] * len(rblocks), axis=0)
    st = {}

    def head_lanes(v):
        cols = []
        for j in range(SSD_HEADS // 2):
            lo = jnp.broadcast_to(v[:, 2 * j:2 * j + 1], (tt, 128))
            hi = jnp.broadcast_to(v[:, 2 * j + 1:2 * j + 2], (tt, 128))
            cols.append(jnp.where(lane < SSD_HEAD_DIM, lo, hi))
        return jnp.concatenate(cols, axis=1)

    def p_norm():
        x = x_ref[0]
        ms = jnp.mean(x * x, axis=-1, keepdims=True)
        st["x"] = x
        st["xn"] = ((x * lax.rsqrt(ms + EPS)) * n1_ref[...]).astype(BF16)

    def p_dt():
        dt_c = _softplus(_dot(st["xn"], wdt_ref[...]) + dtb_ref[...])
        if masked:
            dt_c = dt_c * valid_ref[:, 0:1]
        dt_s[...] = head_lanes(dt_c)
        da = dt_c * a_neg
        da_hi = da.astype(BF16)
        st["da_hi"] = da_hi
        st["da_mid"] = (da - da_hi.astype(F32)).astype(BF16)

    def conv_silu(lo, hi):
        xbuf[HIST:HIST + tt, lo:hi] = _dot(st["xn"], wm_ref[:, C_XBC + lo:C_XBC + hi])
        conv = (convb_ref[:, lo:hi]
                + convw_ref[SSD_CONV - 1:SSD_CONV, lo:hi] * xbuf[HIST:HIST + tt, lo:hi])
        for j in range(SSD_CONV - 1):
            off = HIST - (SSD_CONV - 1) + j
            conv = conv + convw_ref[j:j + 1, lo:hi] * xbuf[off:off + tt, lo:hi]
        xbuf[0:HIST, lo:hi] = xbuf[tt:tt + HIST, lo:hi]
        act = _silu(conv)
        if masked:
            act = act * valid_ref[:, 0:1]
        return act

    def p_xs():
        xs_s[...] = conv_silu(0, SSD_INNER)

    def p_bc():
        bc = conv_silu(SSD_INNER, SSD_CONV_DIM)
        b_s[...] = bc[:, :SSD_BC].astype(BF16)
        c_s[...] = bc[:, SSD_BC:].astype(BF16)

    def p_q():
        cos2, sin2 = rope_ref[0], rope_ref[1]
        qf = _dot(st["xn"], wm_ref[:, C_Q:C_K])
        for sl in heads:
            qh = qf[:, sl]
            qr = qh * cos2 + pltpu.roll(qh, RET_DK // 2, 1) * sin2
            qr_s[:, sl] = qr.astype(BF16)
            qd_s[:, sl] = (qr * dq_t[:, sl]).astype(BF16)

    def p_k():
        cos2, sin2 = rope_ref[0], rope_ref[1]
        kf = _dot(st["xn"], wm_ref[:, C_K:C_V])
        for sl in heads:
            kh = kf[:, sl]
            kr = kh * cos2 + pltpu.roll(kh, RET_DK // 2, 1) * sin2
            kr_s[:, sl] = kr.astype(BF16)
            kd_s[:, sl] = (kr * dk_t[:, sl]).astype(BF16)

    def p_v():
        v_s[...] = _dot(st["xn"], wm_ref[:, C_V:C_G]).astype(BF16)

    def p_g():
        gate_s[...] = _silu(_dot(st["xn"], wm_ref[:, C_G:C_Z])) * retw_ref[...]

    def p_z():
        zs_s[...] = _silu(_dot(st["xn"], wm_ref[:, C_Z:C_XBC]))

    def m_cumsum():
        tril = tril_ref[...]
        st["acol"] = head_lanes(_dot(tril, st["da_hi"]) + _dot(tril, st["da_mid"]))

    def m_cb():
        cb = [[None] * SSD_GROUPS for _ in chunks]
        for c, rows in enumerate(chunks):
            for g in range(SSD_GROUPS):
                brep = jnp.concatenate([b_s[rows, nlanes[g]]] * SSD_HPG, axis=0)
                cb[c][g] = _dot_nt(c_s[rows, nlanes[g]], brep)
        st["cb"] = cb

    def m_scores():
        lhs_o = [[None] * RET_HEADS for _ in rblocks]
        kv = [[None] * RET_HEADS for _ in rblocks]
        for c, rows in enumerate(rblocks):
            for h in range(RET_HEADS):
                sl = heads[h]
                sc = _dot_nt(qr_s[rows, sl], kr_s[rows, sl])
                p = (sc * dmask_ref[h]).astype(BF16)
                lhs_o[c][h] = jnp.concatenate([qd_s[rows, sl], p], axis=1)
                kv[c][h] = _dot_tn(kd_s[rows, sl], v_s[rows, sl])
        st["lhs_o"], st["kv"] = lhs_o, kv

    def m_intra():
        ydiag = [[None] * SSD_GROUPS for _ in chunks]
        dstate = [[None] * SSD_GROUPS for _ in chunks]
        exp_a, exp_last = [], []
        for c, rows in enumerate(chunks):
            acol = st["acol"][rows]
            arow = jnp.sum(jnp.where(row == col, acol, 0.0), axis=0, keepdims=True)
            lmat = jnp.exp(jnp.where(col <= row, acol - arow, -1e30))
            a_last = acol[CHUNK - 1:CHUNK, :]
            exp_a.append(jnp.exp(acol))
            exp_last.append(jnp.exp(a_last))
            xdt = xs_s[rows, :] * dt_s[rows, :]
            xw = (xdt * jnp.exp(a_last - acol)).astype(BF16)
            xdt_b = xdt.astype(BF16)
            for g in range(SSD_GROUPS):
                wgt = (st["cb"][c][g] * lmat[:, glanes[g]]).astype(BF16)
                xrep = jnp.concatenate([xdt_b[:, glanes[g]]] * SSD_HPG, axis=0)
                bd = jnp.where(blk_r == blk_c, xrep, jnp.zeros_like(xrep))
                ydiag[c][g] = _dot(wgt, bd)
                dstate[c][g] = _dot_tn(b_s[rows, nlanes[g]], xw[:, glanes[g]])
        st["ydiag"], st["dstate"] = ydiag, dstate
        st["exp_a"], st["exp_last"] = exp_a, exp_last

    def m_states():
        st_in = [[None] * RET_HEADS for _ in rblocks]
        for h in range(RET_HEADS):
            s = sret[h]
            for c in range(len(rblocks)):
                st_in[c][h] = s.astype(BF16)
                s = cdec_ref[h:h + 1, :] * s + st["kv"][c][h]
            sret[h] = s
        hg_in = [[None] * SSD_GROUPS for _ in chunks]
        for g in range(SSD_GROUPS):
            hg = hssd[g]
            for c in range(n_chunks):
                hg_in[c][g] = hg.astype(BF16)
                hg = st["exp_last"][c][:, glanes[g]] * hg + st["dstate"][c][g]
            hssd[g] = hg
        st["st_in"], st["hg_in"] = st_in, hg_in

    def m_ret_out():
        for c, rows in enumerate(rblocks):
            for h in range(RET_HEADS):
                o = _dot(st["lhs_o"][c][h],
                         jnp.concatenate([st["st_in"][c][h], v_s[rows, heads[h]]], axis=0))
                o = o * lax.rsqrt(jnp.mean(o * o, axis=-1, keepdims=True) + EPS)
                y_s[rows, heads[h]] = (o * gate_s[rows, heads[h]]).astype(BF16)

    def m_proj_ret():
        st["acc"] = st["x"] + _dot(y_s[:, :RET_W], wout_ref[:RET_W, :])

    def m_ssd_out():
        for c, rows in enumerate(chunks):
            for g in range(SSD_GROUPS):
                gl = glanes[g]
                yg = (st["ydiag"][c][g]
                      + _dot(c_s[rows, nlanes[g]], st["hg_in"][c][g]) * st["exp_a"][c][:, gl]
                      + xs_s[rows, gl] * dsk_ref[:, gl])
                yg = yg * zs_s[rows, gl]
                yg = yg * lax.rsqrt(jnp.mean(yg * yg, axis=-1, keepdims=True) + EPS)
                y_s[rows, RET_W + SSD_GW * g:RET_W + SSD_GW * (g + 1)] = (
                    yg * ssdw_ref[:, gl]).astype(BF16)

    def m_proj_ssd():
        out_ref[0] = st["acc"] + _dot(y_s[:, RET_W:], wout_ref[RET_W:, :])

    for stage in (p_norm, p_dt, p_xs, p_bc, p_q, p_k, m_cumsum, p_v, m_cb, p_g,
                  m_scores, p_z, m_intra, m_states, m_ret_out, m_proj_ret,
                  m_ssd_out, m_proj_ssd):
        stage()

    if emit_state:
        so_ref[...] = sret[...]
        ho_ref[...] = hssd[...]
        co_ref[...] = xbuf[0:HIST, :]


def _const_spec(shape, single=True):
    nd = len(shape)
    kw = {"pipeline_mode": pl.Buffered(1)} if single else {}
    return pl.BlockSpec(shape, lambda *_: (0,) * nd, **kw)


def _mixer_call(x, rope, valid, consts, init, *, tt, emit_state):
    b, n, _ = x.shape
    masked = valid is not None
    has_init = init is not None
    nt = n // tt

    in_specs = [pl.BlockSpec((1, tt, D_MODEL), lambda i, j: (i, j, 0)),
                pl.BlockSpec((2, tt, RET_DK), lambda i, j: (0, j, 0))]
    args = [x, rope]
    if masked:
        in_specs.append(pl.BlockSpec((tt, 128), lambda i, j: (j, 0)))
        args.append(valid)
    for c in consts:
        in_specs.append(_const_spec(c.shape))
        args.append(c)
    if has_init:
        for c in init:
            in_specs.append(_const_spec(c.shape))
            args.append(c)

    out_shape = [jax.ShapeDtypeStruct((b, n, D_MODEL), F32)]
    out_specs = [pl.BlockSpec((1, tt, D_MODEL), lambda i, j: (i, j, 0))]
    if emit_state:
        st_shapes = [(RET_HEADS, RET_DK, RET_DK), (SSD_GROUPS, SSD_STATE, SSD_GW),
                     (HIST, SSD_CONV_DIM)]
        for s in st_shapes:
            out_shape.append(jax.ShapeDtypeStruct(s, F32))
            out_specs.append(_const_spec(s, single=False))

    scratch = [pltpu.VMEM((tt, RET_W), BF16)] * 5
    scratch += [pltpu.VMEM((tt, SSD_INNER), F32)] * 4
    scratch += [pltpu.VMEM((tt, SSD_BC), BF16)] * 2
    scratch += [pltpu.VMEM((tt + HIST, SSD_CONV_DIM), F32),
                pltpu.VMEM((tt, MIX_WIDTH), BF16),
                pltpu.VMEM((RET_HEADS, RET_DK, RET_DK), F32),
                pltpu.VMEM((SSD_GROUPS, SSD_STATE, SSD_GW), F32)]

    kern = functools.partial(_mixer_kernel, tt=tt, masked=masked,
                             has_init=has_init, emit_state=emit_state)
    return pl.pallas_call(
        kern,
        grid=(b, nt),
        in_specs=in_specs,
        out_specs=out_specs,
        out_shape=out_shape,
        scratch_shapes=scratch,
        compiler_params=pltpu.CompilerParams(
            dimension_semantics=("arbitrary", "arbitrary"),
            vmem_limit_bytes=VMEM_LIMIT),
        name="mixer_meta" if emit_state else "mixer",
    )(*args)


def _ffn_kernel(h_ref, n2_ref, w1_ref, w2_ref, fn_ref, out_ref):
    h = h_ref[...]
    ms = jnp.mean(h * h, axis=-1, keepdims=True)
    u = ((h * lax.rsqrt(ms + EPS)) * n2_ref[...]).astype(BF16)
    acc = h
    for j in range(D_FF // FFN_SLAB):
        sl = slice(FFN_SLAB * j, FFN_SLAB * (j + 1))
        a = jnp.maximum(_dot(u, w1_ref[:, sl]), 0.0)
        acc = acc + _dot((a * a).astype(BF16), w2_ref[sl, :])
    ms2 = jnp.mean(acc * acc, axis=-1, keepdims=True)
    out_ref[...] = (acc * lax.rsqrt(ms2 + EPS)) * fn_ref[...]


def _ffn_call(h, n2, w1, w2, fn, *, tm):
    n = h.shape[0]
    return pl.pallas_call(
        _ffn_kernel,
        grid=(n // tm,),
        in_specs=[pl.BlockSpec((tm, D_MODEL), lambda i: (i, 0)),
                  _const_spec(n2.shape), _const_spec(w1.shape),
                  _const_spec(w2.shape), _const_spec(fn.shape)],
        out_specs=pl.BlockSpec((tm, D_MODEL), lambda i: (i, 0)),
        out_shape=jax.ShapeDtypeStruct((n, D_MODEL), F32),
        compiler_params=pltpu.CompilerParams(
            dimension_semantics=("arbitrary",),
            vmem_limit_bytes=VMEM_LIMIT),
        name="ffn",
    )(h, n2, w1, w2, fn)


def _rope_tables(pos):
    half = RET_DK // 2
    freqs = ROPE_BASE ** (-np.arange(0, half, dtype=np.float64) / half)
    ang = np.asarray(pos, np.float64)[:, None] * freqs[None, :]
    cos, sin = np.cos(ang), np.sin(ang)
    return np.stack([np.concatenate([cos, cos], axis=-1),
                     np.concatenate([-sin, sin], axis=-1)]).astype(np.float32)


def _decay_tables(rb):
    scale = RET_DK ** -0.5
    log_gamma = np.log1p(-np.exp2(-5.0 - np.arange(RET_HEADS, dtype=np.float64)))
    idx = np.arange(rb, dtype=np.float64)
    later = (idx[None, :] // CHUNK) > (idx[:, None] // CHUNK)
    dmask = scale * np.exp(log_gamma[:, None, None]
                           * np.abs(idx[:, None] - idx[None, :])[None])
    dmask = np.where(later[None], 0.0, dmask)
    dq = scale * np.exp((idx + 1.0)[:, None] * log_gamma[None, :])
    dk = np.exp((rb - 1.0 - idx)[:, None] * log_gamma[None, :])
    cdec = np.exp(rb * log_gamma)[:, None]
    rep = lambda v: np.repeat(v, RET_DK, axis=-1).astype(np.float32)
    return rep(dq), rep(dk), dmask.astype(np.float32), rep(cdec)


def _lane_rep(v, n):
    return jnp.broadcast_to(v[..., None], v.shape + (n,)).reshape(
        v.shape[:-1] + (v.shape[-1] * n,))


def kernel(x, meta_tokens, norm1_w, w_in, ret_norm_w, conv_w, conv_b, dt_bias,
           a_log, d_skip, ssd_norm_w, w_out, norm2_w, w_ff1, w_ff2, final_norm_w):
    b, seq, d = x.shape
    assert d == D_MODEL and w_in.shape[0] == 1, "single-layer block only"
    tt = min(MIXER_TILE, seq)
    assert seq % tt == 0 and tt % CHUNK == 0

    w = w_in[0]
    w_main = w.astype(BF16)
    lane_pad = lambda v: jnp.pad(v, ((0, 0), (0, 128 - SSD_HEADS)))
    w_dt = lane_pad(w_main[:, C_DT:])
    row2 = lambda v: v.reshape(1, -1).astype(F32)
    hpad = lambda v: lane_pad(row2(v))
    hrow = lambda v: _lane_rep(row2(v), SSD_HEAD_DIM)

    def consts(rb, rows):
        dq, dk, dmask, cdec = _decay_tables(rb)
        r = np.arange(rows)
        tril = jnp.asarray((r[None, :] <= r[:, None])
                           & (r[None, :] // CHUNK == r[:, None] // CHUNK), BF16)
        return [row2(norm1_w[0]), w_main, w_dt, row2(ret_norm_w[0]), dq, dk, dmask,
                cdec, tril, conv_w[0].astype(F32), row2(conv_b[0]), hpad(dt_bias[0]),
                hpad(a_log[0]), hrow(d_skip[0]), row2(ssd_norm_w[0]),
                w_out[0].astype(BF16)]

    m_idx = np.arange(CHUNK)
    x_meta = jnp.concatenate(
        [jnp.zeros((PAD, D_MODEL), x.dtype), meta_tokens.astype(x.dtype)])[None]
    valid = np.broadcast_to((m_idx >= PAD).astype(np.float32)[:, None], (CHUNK, 128))
    rope_meta = _rope_tables(m_idx - PAD)
    _, s0, h0, c0 = _mixer_call(x_meta, rope_meta, valid, consts(CHUNK, CHUNK), None,
                                tt=CHUNK, emit_state=True)

    rope = _rope_tables(np.arange(seq) + N_META)
    (h1,) = _mixer_call(x, rope, None, consts(math.gcd(tt, RET_BLOCK), tt),
                        (s0, h0, c0),
                        tt=tt, emit_state=False)

    tm = min(FFN_TILE, b * seq)
    out = _ffn_call(h1.reshape(b * seq, D_MODEL), row2(norm2_w[0]),
                    w_ff1[0].astype(BF16), w_ff2[0].astype(BF16),
                    row2(final_norm_w), tm=tm)
    return out.reshape(b, seq, D_MODEL)
```
